```python
import jax, jax.numpy as jnp
from jax import lax
import numpy as np

D_MODEL = 1024
BATCH = 8
SEQ = 8192
DEPTH = 1
DEC_BATCH = 16
DEC_SEQ = 16
PAST_LEN = 1024

CHUNK = 64
WINDOW = 128
WINDOW_CHUNKS = WINDOW // CHUNK
N_HEADS = 8
N_KV_HEADS = 2
HEAD_DIM = 128
GROUP = N_HEADS // N_KV_HEADS
ATTN_SCALE = HEAD_DIM ** -0.5
D_RNN = D_MODEL
N_RNN_BLOCKS = 8
RNN_BLOCK = D_RNN // N_RNN_BLOCKS
RNN_CONV = 4
RGLRU_C = 8.0
D_FF = ((8 * D_MODEL // 3 + 127) // 128) * 128
FFN_CONV = 3
EPS = 1e-6
Q_W = N_HEADS * HEAD_DIM
KV_W = N_KV_HEADS * HEAD_DIM
IN_W = D_RNN + Q_W + 2 * KV_W + 2 * D_MODEL
SPLITS = [D_RNN, D_RNN + Q_W, D_RNN + Q_W + KV_W, D_RNN + Q_W + 2 * KV_W]

kernel_name = 'hawk_swa_sink_convffn_stream_step'


def rms_norm(x, g):
    xf = x.astype(jnp.float32)
    r = lax.rsqrt(jnp.mean(xf * xf, axis=-1, keepdims=True) + EPS)
    return (xf * r * g.astype(jnp.float32)).astype(x.dtype)


def alibi_slopes():
    return jnp.asarray(np.array([2.0 ** (-8.0 * (h + 1) / N_HEADS) for h in range(N_HEADS)], dtype=np.float32))


def causal_dwconv(x, prev, w, b):
    width = w.shape[0]
    T = x.shape[1]
    xf = jnp.concatenate([prev.astype(x.dtype), x], axis=1)
    y = xf[:, 0:T] * w[0]
    for j in range(1, width):
        y = y + xf[:, j:j + T] * w[j]
    y = y + b
    return y, xf[:, xf.shape[1] - (width - 1):]


def linear_combine(c1, c2):
    a1, b1 = c1
    a2, b2 = c2
    return a1 * a2, a2 * b1 + b2


def rglru(u, h_prev, w_a, b_a, w_x, b_x, lam):
    B, T, _ = u.shape
    ub = u.reshape(B, T, N_RNN_BLOCKS, RNN_BLOCK)
    r = jax.nn.sigmoid((jnp.einsum('btnc,ncd->btnd', ub, w_a).reshape(B, T, D_RNN) + b_a).astype(jnp.float32))
    i = jax.nn.sigmoid((jnp.einsum('btnc,ncd->btnd', ub, w_x).reshape(B, T, D_RNN) + b_x).astype(jnp.float32))
    log_a = -RGLRU_C * r * jax.nn.softplus(-lam.astype(jnp.float32))
    a = jnp.exp(log_a)
    b = jnp.sqrt(-jnp.expm1(2.0 * log_a)) * (i * u.astype(jnp.float32))
    b = b.at[:, 0].add(a[:, 0] * h_prev.astype(jnp.float32))
    _, h = lax.associative_scan(linear_combine, (a, b), axis=1)
    return h.astype(u.dtype), h[:, -1].astype(h_prev.dtype)


def sink_alibi_attention(q, k, v, q_pos, k_pos, k_valid, sinks, slopes):
    B, N, Lq = q.shape[:3]
    qg = q.reshape(B, N, Lq, N_KV_HEADS, GROUP, HEAD_DIM)
    s = jnp.einsum('bnqkgd,bnskd->bnkgqs', qg, k).astype(jnp.float32) * ATTN_SCALE
    dist = jnp.abs(q_pos[:, :, None] - k_pos[:, None, :]).astype(jnp.float32)
    m = slopes.reshape(N_KV_HEADS, GROUP)
    s = s - m[None, None, :, :, None, None] * dist[None, :, None, None, :, :]
    s = jnp.where(k_valid[None, :, None, None, None, :], s, -jnp.inf)
    sink = sinks.astype(jnp.float32).reshape(N_KV_HEADS, GROUP)[None, None, :, :, None, None]
    mx = jnp.maximum(jnp.max(s, axis=-1, keepdims=True), sink)
    p = jnp.exp(s - mx)
    denom = jnp.sum(p, axis=-1, keepdims=True) + jnp.exp(sink - mx)
    p = (p / denom).astype(v.dtype)
    o = jnp.einsum('bnkgqs,bnskd->bnqkgd', p, v)
    return o.reshape(B, N, Lq, N_HEADS * HEAD_DIM)


def banded_window_attention(q, k, v, sinks, slopes):
    B, T = q.shape[:2]
    n_c = T // CHUNK
    pad = WINDOW_CHUNKS * CHUNK
    span = (WINDOW_CHUNKS + 1) * CHUNK

    def blocks(t):
        tp = jnp.pad(t, ((0, 0), (pad, 0), (0, 0), (0, 0))).reshape(B, n_c + WINDOW_CHUNKS, CHUNK, N_KV_HEADS, HEAD_DIM)
        return jnp.concatenate([tp[:, j:j + n_c] for j in range(WINDOW_CHUNKS + 1)], axis=2)

    qb = q.reshape(B, n_c, CHUNK, N_HEADS, HEAD_DIM)
    q_pos = jnp.arange(T, dtype=jnp.int32).reshape(n_c, CHUNK)
    k_pos = jnp.arange(n_c, dtype=jnp.int32)[:, None] * CHUNK - pad + jnp.arange(span, dtype=jnp.int32)[None, :]
    o = sink_alibi_attention(qb, blocks(k), blocks(v), q_pos, k_pos, k_pos >= 0, sinks, slopes)
    return o.reshape(B, T, N_HEADS * HEAD_DIM)


def cached_window_attention(q, k, v, k_cache, v_cache, sinks, slopes):
    S = q.shape[1]
    n_past = k_cache.shape[1]
    k_all = jnp.concatenate([k_cache.astype(k.dtype), k], axis=1)
    v_all = jnp.concatenate([v_cache.astype(v.dtype), v], axis=1)
    q_pos = (PAST_LEN + jnp.arange(S, dtype=jnp.int32))[None]
    k_pos = (PAST_LEN - n_past + jnp.arange(n_past + S, dtype=jnp.int32))[None]
    valid = jnp.ones(k_pos.shape, dtype=bool)
    o = sink_alibi_attention(q[:, None], k_all[:, None], v_all[:, None], q_pos, k_pos, valid, sinks, slopes)
    L = k_all.shape[1]
    return o[:, 0], k_all[:, L - n_past:], v_all[:, L - n_past:]


def hybrid_layer(x, p, slopes, rnn_conv_prev, rnn_h_prev, ffn_conv_prev, k_cache, v_cache):
    B, T, _ = x.shape
    h = rms_norm(x, p['norm_mix_g'])
    u, q, k, v, gl = jnp.split(h @ p['w_in'], SPLITS, axis=-1)
    u, new_rnn_conv = causal_dwconv(u, rnn_conv_prev, p['rnn_conv_w'], p['rnn_conv_b'])
    y_rnn, new_h = rglru(u, rnn_h_prev, p['rnn_gate_a_w'], p['rnn_gate_a_b'], p['rnn_gate_x_w'], p['rnn_gate_x_b'], p['rnn_lambda'])
    q = rms_norm(q.reshape(B, T, N_HEADS, HEAD_DIM), p['q_norm_g'])
    k = rms_norm(k.reshape(B, T, N_KV_HEADS, HEAD_DIM), p['k_norm_g'])
    v = v.reshape(B, T, N_KV_HEADS, HEAD_DIM)
    if k_cache is None:
        y_attn = banded_window_attention(q, k, v, p['attn_sinks'], slopes)
        new_k, new_v = k[:, T - WINDOW:], v[:, T - WINDOW:]
    else:
        y_attn, new_k, new_v = cached_window_attention(q, k, v, k_cache, v_cache, p['attn_sinks'], slopes)
    g_rnn, g_attn = jnp.split(jax.nn.sigmoid(gl + p['b_gate']), 2, axis=-1)
    mixed = g_rnn * (y_rnn @ p['w_rnn_proj']) + g_attn * (y_attn @ p['w_attn_proj'])
    x = x + mixed @ p['w_out']
    h2 = rms_norm(x, p['norm_ffn_g'])
    a, b = jnp.split(h2 @ p['w_up'], 2, axis=-1)
    a, new_ffn_conv = causal_dwconv(a, ffn_conv_prev, p['ffn_conv_w'], p['ffn_conv_b'])
    x = x + (jax.nn.gelu(a, approximate=False) * b) @ p['w_down']
    return x, (new_rnn_conv, new_h, new_k, new_v, new_ffn_conv)


def setup_inputs(seed: int = 0) -> dict:
    key = jax.random.key(seed)
    ks = iter(jax.random.split(key, 40))

    def nrm(shape, scale):
        return jax.random.normal(next(ks), shape, jnp.float32) * scale

    a0 = jax.random.uniform(next(ks), (DEPTH, D_RNN), jnp.float32, minval=0.9, maxval=0.999)
    return {
        'x_prompt': nrm((BATCH, SEQ, D_MODEL), 1.0),
        'x_sample': nrm((DEC_BATCH, DEC_SEQ, D_MODEL), 1.0),
        'state_rnn_conv': nrm((DEPTH, DEC_BATCH, RNN_CONV - 1, D_RNN), 1.0),
        'state_rnn_h': nrm((DEPTH, DEC_BATCH, D_RNN), 0.5),
        'cache_attn_k': nrm((DEPTH, DEC_BATCH, min(WINDOW, PAST_LEN), N_KV_HEADS, HEAD_DIM), 1.0),
        'cache_attn_v': nrm((DEPTH, DEC_BATCH, min(WINDOW, PAST_LEN), N_KV_HEADS, HEAD_DIM), 1.0),
        'state_ffn_conv': nrm((DEPTH, DEC_BATCH, FFN_CONV - 1, D_FF), 1.0),
        'norm_mix_g': 1.0 + nrm((DEPTH, D_MODEL), 0.02),
        'w_in': nrm((DEPTH, D_MODEL, IN_W), D_MODEL ** -0.5),
        'b_gate': nrm((DEPTH, 2 * D_MODEL), 0.02),
        'rnn_conv_w': nrm((DEPTH, RNN_CONV, D_RNN), RNN_CONV ** -0.5),
        'rnn_conv_b': nrm((DEPTH, D_RNN), 0.02),
        'rnn_gate_a_w': nrm((DEPTH, N_RNN_BLOCKS, RNN_BLOCK, RNN_BLOCK), RNN_BLOCK ** -0.5),
        'rnn_gate_a_b': nrm((DEPTH, D_RNN), 0.02),
        'rnn_gate_x_w': nrm((DEPTH, N_RNN_BLOCKS, RNN_BLOCK, RNN_BLOCK), RNN_BLOCK ** -0.5),
        'rnn_gate_x_b': nrm((DEPTH, D_RNN), 0.02),
        'rnn_lambda': jnp.log(a0) - jnp.log1p(-a0),
        'q_norm_g': 1.0 + nrm((DEPTH, HEAD_DIM), 0.02),
        'k_norm_g': 1.0 + nrm((DEPTH, HEAD_DIM), 0.02),
        'attn_sinks': nrm((DEPTH, N_HEADS), 0.5),
        'w_rnn_proj': nrm((DEPTH, D_RNN, D_MODEL), D_RNN ** -0.5),
        'w_attn_proj': nrm((DEPTH, Q_W, D_MODEL), Q_W ** -0.5),
        'w_out': nrm((DEPTH, D_MODEL, D_MODEL), D_MODEL ** -0.5),
        'norm_ffn_g': 1.0 + nrm((DEPTH, D_MODEL), 0.02),
        'w_up': nrm((DEPTH, D_MODEL, 2 * D_FF), D_MODEL ** -0.5),
        'ffn_conv_w': nrm((DEPTH, FFN_CONV, D_FF), FFN_CONV ** -0.5),
        'ffn_conv_b': nrm((DEPTH, D_FF), 0.02),
        'w_down': nrm((DEPTH, D_FF, D_MODEL), D_FF ** -0.5),
    }


def reference(x_prompt, x_sample, state_rnn_conv, state_rnn_h, cache_attn_k, cache_attn_v, state_ffn_conv,
              norm_mix_g, w_in, b_gate, rnn_conv_w, rnn_conv_b, rnn_gate_a_w, rnn_gate_a_b, rnn_gate_x_w,
              rnn_gate_x_b, rnn_lambda, q_norm_g, k_norm_g, attn_sinks, w_rnn_proj, w_attn_proj, w_out,
              norm_ffn_g, w_up, ffn_conv_w, ffn_conv_b, w_down):
    B = x_prompt.shape[0]
    dt = x_prompt.dtype
    slopes = alibi_slopes()
    xp, xs = x_prompt, x_sample
    st_p = ([], [], [], [], [])
    st_s = ([], [], [], [], [])
    for l in range(DEPTH):
        p = {
            'norm_mix_g': norm_mix_g[l], 'w_in': w_in[l], 'b_gate': b_gate[l],
            'rnn_conv_w': rnn_conv_w[l], 'rnn_conv_b': rnn_conv_b[l],
            'rnn_gate_a_w': rnn_gate_a_w[l], 'rnn_gate_a_b': rnn_gate_a_b[l],
            'rnn_gate_x_w': rnn_gate_x_w[l], 'rnn_gate_x_b': rnn_gate_x_b[l],
            'rnn_lambda': rnn_lambda[l], 'q_norm_g': q_norm_g[l], 'k_norm_g': k_norm_g[l],
            'attn_sinks': attn_sinks[l], 'w_rnn_proj': w_rnn_proj[l], 'w_attn_proj': w_attn_proj[l],
            'w_out': w_out[l], 'norm_ffn_g': norm_ffn_g[l], 'w_up': w_up[l],
            'ffn_conv_w': ffn_conv_w[l], 'ffn_conv_b': ffn_conv_b[l], 'w_down': w_down[l],
        }
        xp, sp = hybrid_layer(
            xp, p, slopes,
            jnp.zeros((B, RNN_CONV - 1, D_RNN), dt),
            jnp.zeros((B, D_RNN), dt),
            jnp.zeros((B, FFN_CONV - 1, D_FF), dt),
            None, None)
        xs, ss = hybrid_layer(
            xs, p, slopes, state_rnn_conv[l], state_rnn_h[l], state_ffn_conv[l],
            cache_attn_k[l], cache_attn_v[l])
        for j in range(5):
            st_p[j].append(sp[j])
            st_s[j].append(ss[j])
    return (xp, xs,
            jnp.stack(st_p[0]), jnp.stack(st_s[0]),
            jnp.stack(st_p[1]), jnp.stack(st_s[1]),
            jnp.stack(st_p[2]), jnp.stack(st_s[2]),
            jnp.stack(st_p[3]), jnp.stack(st_s[3]),
            jnp.stack(st_p[4]), jnp.stack(st_s[4]))
```

```python
import functools
import math

import jax
import jax.numpy as jnp
import numpy as np
from jax import lax
from jax.experimental import pallas as pl
from jax.experimental.pallas import tpu as pltpu

D_MODEL = 1024
CHUNK = 64
WINDOW = 128
N_HEADS = 8
N_KV_HEADS = 2
HEAD_DIM = 128
GROUP = N_HEADS // N_KV_HEADS
ATTN_SCALE = HEAD_DIM ** -0.5
D_RNN = D_MODEL
N_RNN_BLOCKS = 8
RNN_BLOCK = D_RNN // N_RNN_BLOCKS
RNN_CONV = 4
RGLRU_C = 8.0
D_FF = ((8 * D_MODEL // 3 + 127) // 128) * 128
FFN_CONV = 3
EPS = 1e-6
Q_W = N_HEADS * HEAD_DIM
KV_W = N_KV_HEADS * HEAD_DIM
IN_W = D_RNN + Q_W + 2 * KV_W + 2 * D_MODEL
U_OFF, Q_OFF, K_OFF, V_OFF, G_OFF = 0, D_RNN, D_RNN + Q_W, D_RNN + Q_W + KV_W, D_RNN + Q_W + 2 * KV_W

SUBLANES = 8
HIST = SUBLANES
VMEM_LIMIT_BYTES = 56 * 1024 * 1024

F32 = jnp.float32
BF16 = jnp.bfloat16


def _sigmoid(x):
    return 1.0 / (1.0 + jnp.exp(-x))


def _rms_rows(x, g):
    r = lax.rsqrt(jnp.mean(x * x, axis=-1, keepdims=True) + EPS)
    return x * r * g


def _dot(a, b):
    return jnp.dot(a, b, preferred_element_type=F32)


def _dot_nt(a, b):
    return lax.dot_general(a, b, (((1,), (1,)), ((), ())), preferred_element_type=F32)


def _mixer_kernel(*refs, nb, ts, lq, n_chunks, sample):
    if sample:
        (x_ref, sconv_ref, sh_ref, kc_ref, vc_ref, *rest) = refs
    else:
        (x_ref, *rest) = refs
    (g1_ref, w_in_ref, bgate_ref, cw_ref, cb_ref, wg_ref, ba_ref, bx_ref, lam_ref, qg_ref, kg_ref,
     sinks_ref, bias_ref, wr_ref, wa_ref, wo_ref,
     x1_ref, conv_out_ref, h_out_ref, k_out_ref, v_out_ref,
     ubuf, abuf, bbuf, hcar, kbuf, vbuf, qbuf, yabuf, mixbuf) = rest

    rows = nb * ts
    t = pl.program_id(1)

    if not sample:
        @pl.when(t == 0)
        def _():
            ubuf[:, 0:HIST, :] = jnp.zeros((nb, HIST, D_RNN), F32)
            hcar[...] = jnp.zeros_like(hcar)
            kbuf[:, 0:WINDOW, :] = jnp.zeros((nb, WINDOW, KV_W), BF16)
            vbuf[:, 0:WINDOW, :] = jnp.zeros((nb, WINDOW, KV_W), BF16)

    x = x_ref[...].reshape(rows, D_MODEL)
    hb = _rms_rows(x, g1_ref[...]).astype(BF16)

    u = _dot(hb, w_in_ref[:, U_OFF:U_OFF + D_RNN])
    for s in range(nb):
        if sample:
            ubuf[s, HIST - (RNN_CONV - 1):HIST, :] = sconv_ref[s]
        ubuf[s, HIST:HIST + ts, :] = u[s * ts:(s + 1) * ts]
    ucs = []
    for s in range(nb):
        acc = ubuf[s, HIST:HIST + ts, :] * cw_ref[RNN_CONV - 1:RNN_CONV, :]
        for j in range(RNN_CONV - 1):
            off = HIST - (RNN_CONV - 1) + j
            acc = acc + ubuf[s, off:off + ts, :] * cw_ref[j:j + 1, :]
        ucs.append(acc + cb_ref[...])
        conv_out_ref[s] = ubuf[s, HIST + ts - (RNN_CONV - 1):HIST + ts, :]
        if not sample:
            ubuf[s, 0:HIST, :] = ubuf[s, ts:ts + HIST, :]
    uc = ucs[0] if nb == 1 else jnp.concatenate(ucs, axis=0)
    ucb = uc.astype(BF16)

    lam = lam_ref[...]
    neg_c_softplus = -RGLRU_C * (jnp.maximum(-lam, 0.0) + jnp.log1p(jnp.exp(-jnp.abs(lam))))
    for n in range(N_RNN_BLOCKS):
        sl = slice(n * RNN_BLOCK, (n + 1) * RNN_BLOCK)
        z = _dot(ucb[:, sl], wg_ref[n])
        r = _sigmoid(z[:, :RNN_BLOCK] + ba_ref[:, sl])
        i = _sigmoid(z[:, RNN_BLOCK:] + bx_ref[:, sl])
        a = jnp.exp(neg_c_softplus[:, sl] * r)
        abuf[:, sl] = a
        bbuf[:, sl] = jnp.sqrt(1.0 - a * a) * (i * uc[:, sl])

    row_id = lax.broadcasted_iota(jnp.int32, (SUBLANES, D_RNN), 0)
    groups = ts // SUBLANES
    for s in range(nb):
        if sample:
            h0 = jnp.broadcast_to(sh_ref[s], (SUBLANES, D_RNN))
        else:
            h0 = hcar[s]

        def scan_group(g, carry, s=s):
            base = pl.multiple_of(s * ts + g * SUBLANES, SUBLANES)
            a = abuf[pl.ds(base, SUBLANES), :]
            b = bbuf[pl.ds(base, SUBLANES), :]
            for sh in (1, 2, 4):
                m = row_id >= sh
                b = jnp.where(m, a * pltpu.roll(b, sh, 0) + b, b)
                a = jnp.where(m, a * pltpu.roll(a, sh, 0), a)
            h = a * carry + b
            bbuf[pl.ds(base, SUBLANES), :] = h
            return jnp.broadcast_to(h[SUBLANES - 1:SUBLANES, :], (SUBLANES, D_RNN))

        hl = lax.fori_loop(0, groups, scan_group, h0, unroll=2)
        h_out_ref[s] = hl[0:1, :]
        if not sample:
            hcar[s] = hl

    g_rnn = _sigmoid(_dot(hb, w_in_ref[:, G_OFF:G_OFF + D_MODEL]) + bgate_ref[:, 0:D_MODEL])
    mixbuf[...] = g_rnn * _dot(bbuf[...].astype(BF16), wr_ref[...])

    q = _dot(hb, w_in_ref[:, Q_OFF:Q_OFF + Q_W])
    for hd in range(N_HEADS):
        sl = slice(hd * HEAD_DIM, (hd + 1) * HEAD_DIM)
        qbuf[:, sl] = _rms_rows(q[:, sl], qg_ref[...]).astype(BF16)
    k = _dot(hb, w_in_ref[:, K_OFF:K_OFF + KV_W])
    v = _dot(hb, w_in_ref[:, V_OFF:V_OFF + KV_W])
    kn = jnp.concatenate(
        [_rms_rows(k[:, j * HEAD_DIM:(j + 1) * HEAD_DIM], kg_ref[...]) for j in range(N_KV_HEADS)], axis=1)
    for s in range(nb):
        if sample:
            kbuf[s, 0:WINDOW, :] = kc_ref[s].astype(BF16)
            vbuf[s, 0:WINDOW, :] = vc_ref[s].astype(BF16)
            k_out_ref[s, 0:WINDOW - ts, :] = kc_ref[s, ts:WINDOW, :]
            v_out_ref[s, 0:WINDOW - ts, :] = vc_ref[s, ts:WINDOW, :]
            k_out_ref[s, WINDOW - ts:WINDOW, :] = kn[s * ts:(s + 1) * ts]
            v_out_ref[s, WINDOW - ts:WINDOW, :] = v[s * ts:(s + 1) * ts]
        else:
            k_out_ref[s] = kn[(s + 1) * ts - WINDOW:(s + 1) * ts]
            v_out_ref[s] = v[(s + 1) * ts - WINDOW:(s + 1) * ts]
        kbuf[s, WINDOW:WINDOW + ts, :] = kn[s * ts:(s + 1) * ts].astype(BF16)
        vbuf[s, WINDOW:WINDOW + ts, :] = v[s * ts:(s + 1) * ts].astype(BF16)

    lk = WINDOW + lq
    col_id = lax.broadcasted_iota(jnp.int32, (lq, lk), 1)
    for s in range(nb):
        for c in range(n_chunks):
            r0 = s * ts + c * lq
            for kv in range(N_KV_HEADS):
                ksl = slice(kv * HEAD_DIM, (kv + 1) * HEAD_DIM)
                kk = kbuf[s, c * lq:c * lq + lk, ksl]
                vv = vbuf[s, c * lq:c * lq + lk, ksl]
                qs = jnp.concatenate(
                    [qbuf[r0:r0 + lq, (kv * GROUP + g) * HEAD_DIM:(kv * GROUP + g + 1) * HEAD_DIM]
                     for g in range(GROUP)], axis=0)
                sc = _dot_nt(qs, kk) * ATTN_SCALE
                ps = []
                dens = []
                for g in range(GROUP):
                    hd = kv * GROUP + g
                    sg = sc[g * lq:(g + 1) * lq] - bias_ref[hd]
                    if (not sample) and c * lq < WINDOW:
                        first_valid = jnp.where(t == 0, WINDOW - c * lq, 0)
                        sg = jnp.where(col_id >= first_valid, sg, -jnp.inf)
                    sink = sinks_ref[hd]
                    mx = jnp.maximum(jnp.max(sg, axis=-1, keepdims=True), sink)
                    p = jnp.exp(sg - mx)
                    dens.append(jnp.sum(p, axis=-1, keepdims=True) + jnp.exp(sink - mx))
                    ps.append(p.astype(BF16))
                o = _dot(jnp.concatenate(ps, axis=0), vv)
                for g in range(GROUP):
                    hd = kv * GROUP + g
                    yabuf[r0:r0 + lq, hd * HEAD_DIM:(hd + 1) * HEAD_DIM] = (
                        o[g * lq:(g + 1) * lq] / dens[g]).astype(BF16)
        if not sample:
            kbuf[s, 0:WINDOW, :] = kbuf[s, ts:ts + WINDOW, :]
            vbuf[s, 0:WINDOW, :] = vbuf[s, ts:ts + WINDOW, :]

    g_attn = _sigmoid(_dot(hb, w_in_ref[:, G_OFF + D_MODEL:G_OFF + 2 * D_MODEL]) + bgate_ref[:, D_MODEL:2 * D_MODEL])
    mixed = mixbuf[...] + g_attn * _dot(yabuf[...], wa_ref[...])
    out = x + _dot(mixed.astype(BF16), wo_ref[...])
    x1_ref[...] = out.reshape(nb, ts, D_MODEL)


def _ffn_kernel(*refs, nb, ts, sample):
    if sample:
        (x_ref, sconv_ref, *rest) = refs
    else:
        (x_ref, *rest) = refs
    (g2_ref, wup_ref, cw_ref, cb_ref, wdn_ref, y_ref, conv_out_ref, abuf, hbuf) = rest
    rows = nb * ts
    t = pl.program_id(1)

    if not sample:
        @pl.when(t == 0)
        def _():
            abuf[:, 0:HIST, :] = jnp.zeros((nb, HIST, D_FF), F32)

    x = x_ref[...].reshape(rows, D_MODEL)
    hb = _rms_rows(x, g2_ref[...]).astype(BF16)
    a = _dot(hb, wup_ref[:, 0:D_FF])
    for s in range(nb):
        if sample:
            abuf[s, HIST - (FFN_CONV - 1):HIST, :] = sconv_ref[s]
        abuf[s, HIST:HIST + ts, :] = a[s * ts:(s + 1) * ts]
    acs = []
    for s in range(nb):
        acc = abuf[s, HIST:HIST + ts, :] * cw_ref[FFN_CONV - 1:FFN_CONV, :]
        for j in range(FFN_CONV - 1):
            off = HIST - (FFN_CONV - 1) + j
            acc = acc + abuf[s, off:off + ts, :] * cw_ref[j:j + 1, :]
        acs.append(acc + cb_ref[...])
        conv_out_ref[s] = abuf[s, HIST + ts - (FFN_CONV - 1):HIST + ts, :]
        if not sample:
            abuf[s, 0:HIST, :] = abuf[s, ts:ts + HIST, :]
    ac = acs[0] if nb == 1 else jnp.concatenate(acs, axis=0)
    b = _dot(hb, wup_ref[:, D_FF:2 * D_FF])
    gelu = 0.5 * ac * (1.0 + lax.erf(ac * np.float32(math.sqrt(0.5))))
    hbuf[...] = (gelu * b).astype(BF16)
    out = x + _dot(hbuf[...], wdn_ref[...])
    y_ref[...] = out.reshape(nb, ts, D_MODEL)


def _const_spec(shape):
    nd = len(shape)
    return pl.BlockSpec(shape, lambda b, t: (0,) * nd, pipeline_mode=pl.Buffered(1))


def _alibi_bias(lq):
    slopes = np.array([2.0 ** (-8.0 * (h + 1) / N_HEADS) for h in range(N_HEADS)], dtype=np.float32)
    i = np.arange(lq, dtype=np.int32)[:, None]
    j = np.arange(WINDOW + lq, dtype=np.int32)[None, :]
    dist = np.abs(i + WINDOW - j).astype(np.float32)
    return jnp.asarray(slopes[:, None, None] * dist[None])


def _mixer_call(x, state, w, *, ts, sample):
    bsz, seq, _ = x.shape
    if sample:
        nb, grid, lq, n_chunks = bsz, (1, 1), seq, 1
    else:
        nb, grid, lq, n_chunks = 1, (bsz, seq // ts), CHUNK, ts // CHUNK
    rows = nb * ts
    tile = lambda b, t: (b, t, 0)
    per_b = lambda b, t: (b, 0, 0)
    in_specs = [pl.BlockSpec((nb, ts, D_MODEL), tile)]
    args = [x]
    if sample:
        sconv, sh, kc, vc = state
        in_specs += [pl.BlockSpec((nb, RNN_CONV - 1, D_RNN), per_b), pl.BlockSpec((nb, 1, D_RNN), per_b),
                     pl.BlockSpec((nb, WINDOW, KV_W), per_b), pl.BlockSpec((nb, WINDOW, KV_W), per_b)]
        args += [sconv, sh.reshape(bsz, 1, D_RNN), kc.reshape(bsz, WINDOW, KV_W), vc.reshape(bsz, WINDOW, KV_W)]
    weights = [w['g1'], w['w_in'], w['b_gate'], w['rnn_conv_w'], w['rnn_conv_b'], w['w_gate'], w['b_a'], w['b_x'],
               w['lam'], w['q_g'], w['k_g']]
    in_specs += [_const_spec(a.shape) for a in weights]
    args += weights
    in_specs.append(pl.BlockSpec(memory_space=pltpu.SMEM))
    args.append(w['sinks'])
    tail = [_alibi_bias(lq), w['w_rnn_proj'], w['w_attn_proj'], w['w_out']]
    in_specs += [_const_spec(a.shape) for a in tail]
    args += tail

    out_shape = (jax.ShapeDtypeStruct((bsz, seq, D_MODEL), F32),
                 jax.ShapeDtypeStruct((bsz, RNN_CONV - 1, D_RNN), F32),
                 jax.ShapeDtypeStruct((bsz, 1, D_RNN), F32),
                 jax.ShapeDtypeStruct((bsz, WINDOW, KV_W), F32),
                 jax.ShapeDtypeStruct((bsz, WINDOW, KV_W), F32))
    out_specs = (pl.BlockSpec((nb, ts, D_MODEL), tile),
                 pl.BlockSpec((nb, RNN_CONV - 1, D_RNN), per_b),
                 pl.BlockSpec((nb, 1, D_RNN), per_b),
                 pl.BlockSpec((nb, WINDOW, KV_W), per_b),
                 pl.BlockSpec((nb, WINDOW, KV_W), per_b))
    scratch = [pltpu.VMEM((nb, HIST + ts, D_RNN), F32),
               pltpu.VMEM((rows, D_RNN), F32),
               pltpu.VMEM((rows, D_RNN), F32),
               pltpu.VMEM((nb, SUBLANES, D_RNN), F32),
               pltpu.VMEM((nb, WINDOW + ts, KV_W), BF16),
               pltpu.VMEM((nb, WINDOW + ts, KV_W), BF16),
               pltpu.VMEM((rows, Q_W), BF16),
               pltpu.VMEM((rows, Q_W), BF16),
               pltpu.VMEM((rows, D_MODEL), F32)]
    kern = functools.partial(_mixer_kernel, nb=nb, ts=ts, lq=lq, n_chunks=n_chunks, sample=sample)
    return pl.pallas_call(
        kern, out_shape=out_shape, grid=grid, in_specs=in_specs, out_specs=out_specs, scratch_shapes=scratch,
        name='mixer_sample' if sample else 'mixer_prompt',
        compiler_params=pltpu.CompilerParams(dimension_semantics=('arbitrary', 'arbitrary'),
                                             vmem_limit_bytes=VMEM_LIMIT_BYTES),
    )(*args)


def _ffn_call(x, state, w, *, ts, sample):
    bsz, seq, _ = x.shape
    if sample:
        nb, grid = bsz, (1, 1)
    else:
        nb, grid = 1, (bsz, seq // ts)
    rows = nb * ts
    tile = lambda b, t: (b, t, 0)
    per_b = lambda b, t: (b, 0, 0)
    in_specs = [pl.BlockSpec((nb, ts, D_MODEL), tile)]
    args = [x]
    if sample:
        in_specs.append(pl.BlockSpec((nb, FFN_CONV - 1, D_FF), per_b))
        args.append(state)
    weights = [w['g2'], w['w_up'], w['ffn_conv_w'], w['ffn_conv_b'], w['w_down']]
    in_specs += [_const_spec(a.shape) for a in weights]
    args += weights
    out_shape = (jax.ShapeDtypeStruct((bsz, seq, D_MODEL), F32),
                 jax.ShapeDtypeStruct((bsz, FFN_CONV - 1, D_FF), F32))
    out_specs = (pl.BlockSpec((nb, ts, D_MODEL), tile),
                 pl.BlockSpec((nb, FFN_CONV - 1, D_FF), per_b))
    scratch = [pltpu.VMEM((nb, HIST + ts, D_FF), F32),
               pltpu.VMEM((rows, D_FF), BF16)]
    kern = functools.partial(_ffn_kernel, nb=nb, ts=ts, sample=sample)
    return pl.pallas_call(
        kern, out_shape=out_shape, grid=grid, in_specs=in_specs, out_specs=out_specs, scratch_shapes=scratch,
        name='ffn_sample' if sample else 'ffn_prompt',
        compiler_params=pltpu.CompilerParams(dimension_semantics=('arbitrary', 'arbitrary'),
                                             vmem_limit_bytes=VMEM_LIMIT_BYTES),
    )(*args)


PROMPT_TILE = 256


def kernel(x_prompt, x_sample, state_rnn_conv, state_rnn_h, cache_attn_k, cache_attn_v, state_ffn_conv, norm_mix_g, w_in, b_gate, rnn_conv_w, rnn_conv_b, rnn_gate_a_w, rnn_gate_a_b, rnn_gate_x_w, rnn_gate_x_b, rnn_lambda, q_norm_g, k_norm_g, attn_sinks, w_rnn_proj, w_attn_proj, w_out, norm_ffn_g, w_up, ffn_conv_w, ffn_conv_b, w_down):
    depth = w_in.shape[0]
    xp, xs = x_prompt, x_sample
    outs = [[] for _ in range(10)]
    for l in range(depth):
        w = {
            'g1': norm_mix_g[l][None], 'w_in': w_in[l].astype(BF16), 'b_gate': b_gate[l][None],
            'rnn_conv_w': rnn_conv_w[l], 'rnn_conv_b': rnn_conv_b[l][None],
            'w_gate': jnp.concatenate([rnn_gate_a_w[l], rnn_gate_x_w[l]], axis=-1).astype(BF16),
            'b_a': rnn_gate_a_b[l][None], 'b_x': rnn_gate_x_b[l][None], 'lam': rnn_lambda[l][None],
            'q_g': q_norm_g[l][None], 'k_g': k_norm_g[l][None], 'sinks': attn_sinks[l],
            'w_rnn_proj': w_rnn_proj[l].astype(BF16), 'w_attn_proj': w_attn_proj[l].astype(BF16),
            'w_out': w_out[l].astype(BF16),
            'g2': norm_ffn_g[l][None], 'w_up': w_up[l].astype(BF16), 'ffn_conv_w': ffn_conv_w[l],
            'ffn_conv_b': ffn_conv_b[l][None], 'w_down': w_down[l].astype(BF16),
        }
        xp, conv_p, h_p, k_p, v_p = _mixer_call(xp, None, w, ts=PROMPT_TILE, sample=False)
        xp, fconv_p = _ffn_call(xp, None, w, ts=PROMPT_TILE, sample=False)
        xs, conv_s, h_s, k_s, v_s = _mixer_call(
            xs, (state_rnn_conv[l], state_rnn_h[l], cache_attn_k[l], cache_attn_v[l]), w,
            ts=xs.shape[1], sample=True)
        xs, fconv_s = _ffn_call(xs, state_ffn_conv[l], w, ts=xs.shape[1], sample=True)
        kv_shape = lambda a: a.reshape(a.shape[0], WINDOW, N_KV_HEADS, HEAD_DIM)
        for j, o in enumerate((conv_p, conv_s, h_p[:, 0], h_s[:, 0], kv_shape(k_p), kv_shape(k_s),
                               kv_shape(v_p), kv_shape(v_s), fconv_p, fconv_s)):
            outs[j].append(o)
    return (xp, xs) + tuple(jnp.stack(o) for o in outs)
```

```python
import functools
import math

import jax
import jax.numpy as jnp
import numpy as np
from jax import lax
from jax.experimental import pallas as pl
from jax.experimental.pallas import tpu as pltpu

D_MODEL = 1024
CHUNK = 64
WINDOW = 128
N_HEADS = 8
N_KV_HEADS = 2
HEAD_DIM = 128
GROUP = N_HEADS // N_KV_HEADS
ATTN_SCALE = HEAD_DIM ** -0.5
D_RNN = D_MODEL
N_RNN_BLOCKS = 8
RNN_BLOCK = D_RNN // N_RNN_BLOCKS
RNN_CONV = 4
RGLRU_C = 8.0
D_FF = ((8 * D_MODEL // 3 + 127) // 128) * 128
FFN_CONV = 3
EPS = 1e-6
Q_W = N_HEADS * HEAD_DIM
KV_W = N_KV_HEADS * HEAD_DIM
IN_W = D_RNN + Q_W + 2 * KV_W + 2 * D_MODEL
U_OFF, Q_OFF, K_OFF, V_OFF, G_OFF = 0, D_RNN, D_RNN + Q_W, D_RNN + Q_W + KV_W, D_RNN + Q_W + 2 * KV_W

SUBLANES = 8
MXU_COLS = 256
HIST = SUBLANES
VMEM_LIMIT_BYTES = 56 * 1024 * 1024
NEG_LOG2E = -1.4426950408889634

F32 = jnp.float32
BF16 = jnp.bfloat16


def _sigmoid(x):
    return 1.0 / (1.0 + jnp.exp2(x * NEG_LOG2E))


def _sqrt_nonneg(y):
    return jnp.where(y > 0.0, y * lax.rsqrt(y), 0.0)


def _rms_rows(x, g):
    r = lax.rsqrt(jnp.mean(x * x, axis=-1, keepdims=True) + EPS)
    return x * r * g


def _dot(a, b):
    return jnp.dot(a, b, preferred_element_type=F32)


def _dot_nt(a, b):
    return lax.dot_general(a, b, (((1,), (1,)), ((), ())), preferred_element_type=F32)


def _scan_rows(a, b, h0, row_id):
    outs = []
    carry = h0
    for g in range(a.shape[0] // SUBLANES):
        ag = a[g * SUBLANES:(g + 1) * SUBLANES]
        bg = b[g * SUBLANES:(g + 1) * SUBLANES]
        for sh in (1, 2, 4):
            m = row_id >= sh
            bg = jnp.where(m, ag * pltpu.roll(bg, sh, 0) + bg, bg)
            ag = jnp.where(m, ag * pltpu.roll(ag, sh, 0), ag)
        h = ag * carry + bg
        outs.append(h)
        carry = jnp.broadcast_to(h[SUBLANES - 1:SUBLANES, :], h.shape)
    return (outs[0] if len(outs) == 1 else jnp.concatenate(outs, axis=0)), carry


def _mixer_kernel(*refs, nb, ts, lq, n_chunks, sample):
    if sample:
        (x_ref, sconv_ref, sh_ref, kc_ref, vc_ref, *rest) = refs
    else:
        (x_ref, *rest) = refs
    (g1_ref, w_in_ref, bgate_ref, cw_ref, cb_ref, wg_ref, ba_ref, bx_ref, lam_ref, qg_ref, kg_ref,
     sinks_ref, bias_ref, wr_ref, wa_ref, wo_ref,
     x1_ref, conv_out_ref, h_out_ref, k_out_ref, v_out_ref,
     ubuf, ybuf, hcar, kbuf, vbuf, qbuf, yabuf, glbuf, mixbuf) = rest

    rows = nb * ts
    t = pl.program_id(1)

    if not sample:
        @pl.when(t == 0)
        def _():
            ubuf[:, 0:HIST, :] = jnp.zeros((nb, HIST, D_RNN), F32)
            hcar[...] = jnp.zeros_like(hcar)
            kbuf[:, 0:WINDOW, :] = jnp.zeros((nb, WINDOW, KV_W), BF16)
            vbuf[:, 0:WINDOW, :] = jnp.zeros((nb, WINDOW, KV_W), BF16)

    x = x_ref[...].reshape(rows, D_MODEL)
    hb = _rms_rows(x, g1_ref[...]).astype(BF16)

    lam = lam_ref[...]
    neg_c_softplus = -RGLRU_C * (jnp.maximum(-lam, 0.0) + jnp.log1p(jnp.exp(-jnp.abs(lam))))
    row_id = lax.broadcasted_iota(jnp.int32, (SUBLANES, MXU_COLS), 0)

    def project_q(half):
        q = _dot(hb, w_in_ref[:, Q_OFF + half * 512:Q_OFF + (half + 1) * 512])
        for j in range(4):
            hd = half * 4 + j
            qbuf[:, hd * HEAD_DIM:(hd + 1) * HEAD_DIM] = _rms_rows(
                q[:, j * HEAD_DIM:(j + 1) * HEAD_DIM], qg_ref[...]).astype(BF16)

    def project_kv():
        kvp = _dot(hb, w_in_ref[:, K_OFF:K_OFF + 2 * KV_W])
        kn = jnp.concatenate(
            [_rms_rows(kvp[:, j * HEAD_DIM:(j + 1) * HEAD_DIM], kg_ref[...]) for j in range(N_KV_HEADS)], axis=1)
        v = kvp[:, KV_W:2 * KV_W]
        for s in range(nb):
            if sample:
                kbuf[s, 0:WINDOW, :] = kc_ref[s].astype(BF16)
                vbuf[s, 0:WINDOW, :] = vc_ref[s].astype(BF16)
                k_out_ref[s, 0:WINDOW - ts, :] = kc_ref[s, ts:WINDOW, :]
                v_out_ref[s, 0:WINDOW - ts, :] = vc_ref[s, ts:WINDOW, :]
                k_out_ref[s, WINDOW - ts:WINDOW, :] = kn[s * ts:(s + 1) * ts]
                v_out_ref[s, WINDOW - ts:WINDOW, :] = v[s * ts:(s + 1) * ts]
            else:
                k_out_ref[s] = kn[(s + 1) * ts - WINDOW:(s + 1) * ts]
                v_out_ref[s] = v[(s + 1) * ts - WINDOW:(s + 1) * ts]
            kbuf[s, WINDOW:WINDOW + ts, :] = kn[s * ts:(s + 1) * ts].astype(BF16)
            vbuf[s, WINDOW:WINDOW + ts, :] = v[s * ts:(s + 1) * ts].astype(BF16)

    def project_gate(part):
        glbuf[:, part * 512:(part + 1) * 512] = _dot(
            hb, w_in_ref[:, G_OFF + part * 512:G_OFF + (part + 1) * 512])

    side_work = [[lambda: project_q(0), lambda: project_q(1)],
                 [project_kv, lambda: project_gate(0)],
                 [lambda: project_gate(1), lambda: project_gate(2)],
                 [lambda: project_gate(3)]]

    n_blocks = D_RNN // MXU_COLS
    per_block = MXU_COLS // RNN_BLOCK

    def u_proj(m):
        return _dot(hb, w_in_ref[:, U_OFF + m * MXU_COLS:U_OFF + (m + 1) * MXU_COLS])

    def conv_block(m, u):
        cs = slice(m * MXU_COLS, (m + 1) * MXU_COLS)
        ucs = []
        for s in range(nb):
            if sample:
                ubuf[s, HIST - (RNN_CONV - 1):HIST, cs] = sconv_ref[s, :, cs]
            ubuf[s, HIST:HIST + ts, cs] = u[s * ts:(s + 1) * ts]
            acc = u[s * ts:(s + 1) * ts] * cw_ref[RNN_CONV - 1:RNN_CONV, cs]
            for j in range(RNN_CONV - 1):
                off = HIST - (RNN_CONV - 1) + j
                acc = acc + ubuf[s, off:off + ts, cs] * cw_ref[j:j + 1, cs]
            ucs.append(acc + cb_ref[:, cs])
            conv_out_ref[s, :, cs] = ubuf[s, HIST + ts - (RNN_CONV - 1):HIST + ts, cs]
            if not sample:
                ubuf[s, 0:HIST, cs] = ubuf[s, ts:ts + HIST, cs]
        return ucs[0] if nb == 1 else jnp.concatenate(ucs, axis=0)

    def gate_dots(m, ucb):
        return [_dot(ucb[:, j * RNN_BLOCK:(j + 1) * RNN_BLOCK], wg_ref[m * per_block + j]) for j in range(per_block)]

    def gate_math(m, zs, uc):
        a_parts, b_parts = [], []
        for j, z in enumerate(zs):
            n = m * per_block + j
            sl = slice(n * RNN_BLOCK, (n + 1) * RNN_BLOCK)
            r = _sigmoid(z[:, :RNN_BLOCK] + ba_ref[:, sl])
            i = _sigmoid(z[:, RNN_BLOCK:] + bx_ref[:, sl])
            a = jnp.exp(neg_c_softplus[:, sl] * r)
            a_parts.append(a)
            b_parts.append(_sqrt_nonneg(1.0 - a * a) * (i * uc[:, j * RNN_BLOCK:(j + 1) * RNN_BLOCK]))
        return jnp.concatenate(a_parts, axis=1), jnp.concatenate(b_parts, axis=1)

    def scan_block(m, a_all, b_all):
        cs = slice(m * MXU_COLS, (m + 1) * MXU_COLS)
        for s in range(nb):
            if sample:
                h0 = jnp.broadcast_to(sh_ref[s, :, cs], (SUBLANES, MXU_COLS))
            else:
                h0 = hcar[s, :, cs]
            y, hl = _scan_rows(a_all[s * ts:(s + 1) * ts], b_all[s * ts:(s + 1) * ts], h0, row_id)
            ybuf[s * ts:(s + 1) * ts, cs] = y.astype(BF16)
            h_out_ref[s, :, cs] = hl[0:1, :]
            if not sample:
                hcar[s, :, cs] = hl

    uc = conv_block(0, u_proj(0))
    for m in range(n_blocks):
        zs = gate_dots(m, uc.astype(BF16))
        u_next = u_proj(m + 1) if m + 1 < n_blocks else None
        for work in side_work[m]:
            work()
        a_all, b_all = gate_math(m, zs, uc)
        if u_next is not None:
            uc = conv_block(m + 1, u_next)
        scan_block(m, a_all, b_all)

    lk = WINDOW + lq
    col_id = lax.broadcasted_iota(jnp.int32, (lq, lk), 1)
    units = [(s, c, kv) for s in range(nb) for c in range(n_chunks) for kv in range(N_KV_HEADS)]
    n_pieces = D_MODEL // MXU_COLS
    wr_at = {(len(units) * p) // n_pieces: p for p in range(n_pieces)}

    def rnn_proj(p):
        ps = slice(p * MXU_COLS, (p + 1) * MXU_COLS)
        g_rnn = _sigmoid(glbuf[:, ps] + bgate_ref[:, ps])
        mixbuf[:, ps] = g_rnn * _dot(ybuf[...], wr_ref[:, ps])

    def scores(unit):
        s, c, kv = unit
        r0 = s * ts + c * lq
        kk = kbuf[s, c * lq:c * lq + lk, kv * HEAD_DIM:(kv + 1) * HEAD_DIM]
        qs = jnp.concatenate(
            [qbuf[r0:r0 + lq, (kv * GROUP + g) * HEAD_DIM:(kv * GROUP + g + 1) * HEAD_DIM]
             for g in range(GROUP)], axis=0)
        return _dot_nt(qs, kk)

    def softmax_pv(unit, sc):
        s, c, kv = unit
        r0 = s * ts + c * lq
        vv = vbuf[s, c * lq:c * lq + lk, kv * HEAD_DIM:(kv + 1) * HEAD_DIM]
        sc = sc * ATTN_SCALE
        ps = []
        dens = []
        for g in range(GROUP):
            hd = kv * GROUP + g
            sg = sc[g * lq:(g + 1) * lq] - bias_ref[hd]
            if (not sample) and c * lq < WINDOW:
                first_valid = jnp.where(t == 0, WINDOW - c * lq, 0)
                sg = jnp.where(col_id >= first_valid, sg, -jnp.inf)
            sink = sinks_ref[hd]
            mx = jnp.maximum(jnp.max(sg, axis=-1, keepdims=True), sink)
            p = jnp.exp(sg - mx)
            dens.append(jnp.sum(p, axis=-1, keepdims=True) + jnp.exp(sink - mx))
            ps.append(p.astype(BF16))
        o = _dot(jnp.concatenate(ps, axis=0), vv)
        for g in range(GROUP):
            hd = kv * GROUP + g
            yabuf[r0:r0 + lq, hd * HEAD_DIM:(hd + 1) * HEAD_DIM] = (o[g * lq:(g + 1) * lq] / dens[g]).astype(BF16)

    sc = scores(units[0])
    for idx, unit in enumerate(units):
        sc_next = scores(units[idx + 1]) if idx + 1 < len(units) else None
        if idx in wr_at:
            rnn_proj(wr_at[idx])
        softmax_pv(unit, sc)
        sc = sc_next
    if not sample:
        for s in range(nb):
            kbuf[s, 0:WINDOW, :] = kbuf[s, ts:ts + WINDOW, :]
            vbuf[s, 0:WINDOW, :] = vbuf[s, ts:ts + WINDOW, :]

    acc = None
    for p in range(n_pieces):
        ps = slice(p * MXU_COLS, (p + 1) * MXU_COLS)
        gs = slice(D_MODEL + p * MXU_COLS, D_MODEL + (p + 1) * MXU_COLS)
        g_attn = _sigmoid(glbuf[:, gs] + bgate_ref[:, gs])
        mixed = mixbuf[:, ps] + g_attn * _dot(yabuf[...], wa_ref[:, ps])
        d = _dot(mixed.astype(BF16), wo_ref[ps, :])
        acc = d if acc is None else acc + d
    x1_ref[...] = (x + acc).reshape(nb, ts, D_MODEL)


def _ffn_kernel(*refs, nb, ts, sample):
    if sample:
        (x_ref, sconv_ref, *rest) = refs
    else:
        (x_ref, *rest) = refs
    (g2_ref, wup_ref, cw_ref, cb_ref, wdn_ref, y_ref, conv_out_ref, abuf, hbuf) = rest
    rows = nb * ts
    t = pl.program_id(1)

    if not sample:
        @pl.when(t == 0)
        def _():
            abuf[:, 0:HIST, :] = jnp.zeros((nb, HIST, D_FF), F32)

    x = x_ref[...].reshape(rows, D_MODEL)
    hb = _rms_rows(x, g2_ref[...]).astype(BF16)
    a = _dot(hb, wup_ref[:, 0:D_FF])
    for s in range(nb):
        if sample:
            abuf[s, HIST - (FFN_CONV - 1):HIST, :] = sconv_ref[s]
        abuf[s, HIST:HIST + ts, :] = a[s * ts:(s + 1) * ts]
    acs = []
    for s in range(nb):
        acc = abuf[s, HIST:HIST + ts, :] * cw_ref[FFN_CONV - 1:FFN_CONV, :]
        for j in range(FFN_CONV - 1):
            off = HIST - (FFN_CONV - 1) + j
            acc = acc + abuf[s, off:off + ts, :] * cw_ref[j:j + 1, :]
        acs.append(acc + cb_ref[...])
        conv_out_ref[s] = abuf[s, HIST + ts - (FFN_CONV - 1):HIST + ts, :]
        if not sample:
            abuf[s, 0:HIST, :] = abuf[s, ts:ts + HIST, :]
    ac = acs[0] if nb == 1 else jnp.concatenate(acs, axis=0)
    b = _dot(hb, wup_ref[:, D_FF:2 * D_FF])
    gelu = 0.5 * ac * (1.0 + lax.erf(ac * np.float32(math.sqrt(0.5))))
    hbuf[...] = (gelu * b).astype(BF16)
    out = x + _dot(hbuf[...], wdn_ref[...])
    y_ref[...] = out.reshape(nb, ts, D_MODEL)


def _const_spec(shape):
    nd = len(shape)
    return pl.BlockSpec(shape, lambda b, t: (0,) * nd, pipeline_mode=pl.Buffered(1))


def _alibi_bias(lq):
    slopes = np.array([2.0 ** (-8.0 * (h + 1) / N_HEADS) for h in range(N_HEADS)], dtype=np.float32)
    i = np.arange(lq, dtype=np.int32)[:, None]
    j = np.arange(WINDOW + lq, dtype=np.int32)[None, :]
    dist = np.abs(i + WINDOW - j).astype(np.float32)
    return jnp.asarray(slopes[:, None, None] * dist[None])


def _mixer_call(x, state, w, *, ts, sample):
    bsz, seq, _ = x.shape
    if sample:
        nb, grid, lq, n_chunks = bsz, (1, 1), seq, 1
    else:
        nb, grid, lq, n_chunks = 1, (bsz, seq // ts), CHUNK, ts // CHUNK
    rows = nb * ts
    tile = lambda b, t: (b, t, 0)
    per_b = lambda b, t: (b, 0, 0)
    in_specs = [pl.BlockSpec((nb, ts, D_MODEL), tile)]
    args = [x]
    if sample:
        sconv, sh, kc, vc = state
        in_specs += [pl.BlockSpec((nb, RNN_CONV - 1, D_RNN), per_b), pl.BlockSpec((nb, 1, D_RNN), per_b),
                     pl.BlockSpec((nb, WINDOW, KV_W), per_b), pl.BlockSpec((nb, WINDOW, KV_W), per_b)]
        args += [sconv, sh.reshape(bsz, 1, D_RNN), kc.reshape(bsz, WINDOW, KV_W), vc.reshape(bsz, WINDOW, KV_W)]
    weights = [w['g1'], w['w_in'], w['b_gate'], w['rnn_conv_w'], w['rnn_conv_b'], w['w_gate'], w['b_a'], w['b_x'],
               w['lam'], w['q_g'], w['k_g']]
    in_specs += [_const_spec(a.shape) for a in weights]
    args += weights
    in_specs.append(pl.BlockSpec(memory_space=pltpu.SMEM))
    args.append(w['sinks'])
    tail = [_alibi_bias(lq), w['w_rnn_proj'], w['w_attn_proj'], w['w_out']]
    in_specs += [_const_spec(a.shape) for a in tail]
    args += tail

    out_shape = (jax.ShapeDtypeStruct((bsz, seq, D_MODEL), F32),
                 jax.ShapeDtypeStruct((bsz, RNN_CONV - 1, D_RNN), F32),
                 jax.ShapeDtypeStruct((bsz, 1, D_RNN), F32),
                 jax.ShapeDtypeStruct((bsz, WINDOW, KV_W), F32),
                 jax.ShapeDtypeStruct((bsz, WINDOW, KV_W), F32))
    out_specs = (pl.BlockSpec((nb, ts, D_MODEL), tile),
                 pl.BlockSpec((nb, RNN_CONV - 1, D_RNN), per_b),
                 pl.BlockSpec((nb, 1, D_RNN), per_b),
                 pl.BlockSpec((nb, WINDOW, KV_W), per_b),
                 pl.BlockSpec((nb, WINDOW, KV_W), per_b))
    scratch = [pltpu.VMEM((nb, HIST + ts, D_RNN), F32),
               pltpu.VMEM((rows, D_RNN), BF16),
               pltpu.VMEM((nb, SUBLANES, D_RNN), F32),
               pltpu.VMEM((nb, WINDOW + ts, KV_W), BF16),
               pltpu.VMEM((nb, WINDOW + ts, KV_W), BF16),
               pltpu.VMEM((rows, Q_W), BF16),
               pltpu.VMEM((rows, Q_W), BF16),
               pltpu.VMEM((rows, 2 * D_MODEL), F32),
               pltpu.VMEM((rows, D_MODEL), F32)]
    kern = functools.partial(_mixer_kernel, nb=nb, ts=ts, lq=lq, n_chunks=n_chunks, sample=sample)
    return pl.pallas_call(
        kern, out_shape=out_shape, grid=grid, in_specs=in_specs, out_specs=out_specs, scratch_shapes=scratch,
        name='mixer_sample' if sample else 'mixer_prompt',
        compiler_params=pltpu.CompilerParams(dimension_semantics=('arbitrary', 'arbitrary'),
                                             vmem_limit_bytes=VMEM_LIMIT_BYTES),
    )(*args)


def _ffn_call(x, state, w, *, ts, sample):
    bsz, seq, _ = x.shape
    if sample:
        nb, grid = bsz, (1, 1)
    else:
        nb, grid = 1, (bsz, seq // ts)
    rows = nb * ts
    tile = lambda b, t: (b, t, 0)
    per_b = lambda b, t: (b, 0, 0)
    in_specs = [pl.BlockSpec((nb, ts, D_MODEL), tile)]
    args = [x]
    if sample:
        in_specs.append(pl.BlockSpec((nb, FFN_CONV - 1, D_FF), per_b))
        args.append(state)
    weights = [w['g2'], w['w_up'], w['ffn_conv_w'], w['ffn_conv_b'], w['w_down']]
    in_specs += [_const_spec(a.shape) for a in weights]
    args += weights
    out_shape = (jax.ShapeDtypeStruct((bsz, seq, D_MODEL), F32),
                 jax.ShapeDtypeStruct((bsz, FFN_CONV - 1, D_FF), F32))
    out_specs = (pl.BlockSpec((nb, ts, D_MODEL), tile),
                 pl.BlockSpec((nb, FFN_CONV - 1, D_FF), per_b))
    scratch = [pltpu.VMEM((nb, HIST + ts, D_FF), F32),
               pltpu.VMEM((rows, D_FF), BF16)]
    kern = functools.partial(_ffn_kernel, nb=nb, ts=ts, sample=sample)
    return pl.pallas_call(
        kern, out_shape=out_shape, grid=grid, in_specs=in_specs, out_specs=out_specs, scratch_shapes=scratch,
        name='ffn_sample' if sample else 'ffn_prompt',
        compiler_params=pltpu.CompilerParams(dimension_semantics=('arbitrary', 'arbitrary'),
                                             vmem_limit_bytes=VMEM_LIMIT_BYTES),
    )(*args)


PROMPT_TILE = 256


def kernel(x_prompt, x_sample, state_rnn_conv, state_rnn_h, cache_attn_k, cache_attn_v, state_ffn_conv, norm_mix_g, w_in, b_gate, rnn_conv_w, rnn_conv_b, rnn_gate_a_w, rnn_gate_a_b, rnn_gate_x_w, rnn_gate_x_b, rnn_lambda, q_norm_g, k_norm_g, attn_sinks, w_rnn_proj, w_attn_proj, w_out, norm_ffn_g, w_up, ffn_conv_w, ffn_conv_b, w_down):
    depth = w_in.shape[0]
    xp, xs = x_prompt, x_sample
    outs = [[] for _ in range(10)]
    for l in range(depth):
        w = {
            'g1': norm_mix_g[l][None], 'w_in': w_in[l].astype(BF16), 'b_gate': b_gate[l][None],
            'rnn_conv_w': rnn_conv_w[l], 'rnn_conv_b': rnn_conv_b[l][None],
            'w_gate': jnp.concatenate([rnn_gate_a_w[l], rnn_gate_x_w[l]], axis=-1).astype(BF16),
            'b_a': rnn_gate_a_b[l][None], 'b_x': rnn_gate_x_b[l][None], 'lam': rnn_lambda[l][None],
            'q_g': q_norm_g[l][None], 'k_g': k_norm_g[l][None], 'sinks': attn_sinks[l],
            'w_rnn_proj': w_rnn_proj[l].astype(BF16), 'w_attn_proj': w_attn_proj[l].astype(BF16),
            'w_out': w_out[l].astype(BF16),
            'g2': norm_ffn_g[l][None], 'w_up': w_up[l].astype(BF16), 'ffn_conv_w': ffn_conv_w[l],
            'ffn_conv_b': ffn_conv_b[l][None], 'w_down': w_down[l].astype(BF16),
        }
        xp, conv_p, h_p, k_p, v_p = _mixer_call(xp, None, w, ts=PROMPT_TILE, sample=False)
        xp, fconv_p = _ffn_call(xp, None, w, ts=PROMPT_TILE, sample=False)
        xs, conv_s, h_s, k_s, v_s = _mixer_call(
            xs, (state_rnn_conv[l], state_rnn_h[l], cache_attn_k[l], cache_attn_v[l]), w,
            ts=xs.shape[1], sample=True)
        xs, fconv_s = _ffn_call(xs, state_ffn_conv[l], w, ts=xs.shape[1], sample=True)
        kv_shape = lambda a: a.reshape(a.shape[0], WINDOW, N_KV_HEADS, HEAD_DIM)
        for j, o in enumerate((conv_p, conv_s, h_p[:, 0], h_s[:, 0], kv_shape(k_p), kv_shape(k_s),
                               kv_shape(v_p), kv_shape(v_s), fconv_p, fconv_s)):
            outs[j].append(o)
    return (xp, xs) + tuple(jnp.stack(o) for o in outs)
```

```python
import functools
import math

import jax
import jax.numpy as jnp
import numpy as np
from jax import lax
from jax.experimental import pallas as pl
from jax.experimental.pallas import tpu as pltpu

D_MODEL = 1024
CHUNK = 64
WINDOW = 128
N_HEADS = 8
N_KV_HEADS = 2
HEAD_DIM = 128
GROUP = N_HEADS // N_KV_HEADS
ATTN_SCALE = HEAD_DIM ** -0.5
D_RNN = D_MODEL
N_RNN_BLOCKS = 8
RNN_BLOCK = D_RNN // N_RNN_BLOCKS
RNN_CONV = 4
RGLRU_C = 8.0
D_FF = ((8 * D_MODEL // 3 + 127) // 128) * 128
FFN_CONV = 3
EPS = 1e-6
Q_W = N_HEADS * HEAD_DIM
KV_W = N_KV_HEADS * HEAD_DIM
IN_W = D_RNN + Q_W + 2 * KV_W + 2 * D_MODEL
U_OFF, Q_OFF, K_OFF, V_OFF, G_OFF = 0, D_RNN, D_RNN + Q_W, D_RNN + Q_W + KV_W, D_RNN + Q_W + 2 * KV_W

SUBLANES = 8
MXU_COLS = 256
PHASES = SUBLANES
HIST = SUBLANES
VMEM_LIMIT_BYTES = 56 * 1024 * 1024
NEG_LOG2E = -1.4426950408889634
PROMPT_TILE = 512
CHUNK_SEGS = CHUNK // PHASES

F32 = jnp.float32
BF16 = jnp.bfloat16


def _sigmoid(x):
    return 1.0 / (1.0 + jnp.exp2(x * NEG_LOG2E))


def _sqrt_nonneg(y):
    return jnp.where(y > 0.0, y * lax.rsqrt(y), 0.0)


def _rms_rows(x, g):
    r = lax.rsqrt(jnp.mean(x * x, axis=-1, keepdims=True) + EPS)
    return x * r * g


def _gelu_exact(x):
    return 0.5 * x * (1.0 + lax.erf(x * np.float32(math.sqrt(0.5))))


def _dot(a, b):
    return jnp.dot(a, b, preferred_element_type=F32)


def _dot_nt(a, b):
    return lax.dot_general(a, b, (((1,), (1,)), ((), ())), preferred_element_type=F32)


def _cat_rows(parts):
    return parts[0] if len(parts) == 1 else jnp.concatenate(parts, axis=0)


def _row_groups(v):
    return [v[k * SUBLANES:(k + 1) * SUBLANES] for k in range(v.shape[0] // SUBLANES)]


def _scan_rows(a, b, h0, row_id):
    outs = []
    carry = h0
    for ag, bg in zip(_row_groups(a), _row_groups(b)):
        for sh in (1, 2, 4):
            m = row_id >= sh
            bg = jnp.where(m, ag * pltpu.roll(bg, sh, 0) + bg, bg)
            ag = jnp.where(m, ag * pltpu.roll(ag, sh, 0), ag)
        h = ag * carry + bg
        outs.append(h)
        carry = jnp.broadcast_to(h[SUBLANES - 1:SUBLANES, :], h.shape)
    return _cat_rows(outs), carry


def _shift_rows(v, prev_group, row_id):
    rolled = [pltpu.roll(g, 1, 0) for g in [prev_group] + _row_groups(v)]
    return _cat_rows([jnp.where(row_id >= 1, rolled[k + 1], rolled[k]) for k in range(len(rolled) - 1)])


def _piece_to_chunk_order(v, g_rows):
    return _cat_rows([v[j * g_rows + c * SUBLANES + 0:j * g_rows + (c + 1) * SUBLANES]
                      for c in range(g_rows // CHUNK_SEGS) for j in range(PHASES)])


def _chunk_to_piece_order(v, g_rows):
    return _cat_rows([v[c * CHUNK + j * SUBLANES:c * CHUNK + (j + 1) * SUBLANES]
                      for j in range(PHASES) for c in range(g_rows // CHUNK_SEGS)])


def _softplus_neg(lam):
    return jnp.maximum(-lam, 0.0) + jnp.log1p(jnp.exp(-jnp.abs(lam)))


def _gate_math(z, uc_cols, ba, bx, neg_c_softplus):
    r = _sigmoid(z[:, :RNN_BLOCK] + ba)
    i = _sigmoid(z[:, RNN_BLOCK:] + bx)
    a = jnp.exp(neg_c_softplus * r)
    return a, _sqrt_nonneg(1.0 - a * a) * (i * uc_cols)


def _attend(sc, vv, bias_ref, sinks_ref, kv, lq, first_valid, col_id):
    sc = sc * ATTN_SCALE
    ps, dens = [], []
    for g in range(GROUP):
        hd = kv * GROUP + g
        sg = sc[g * lq:(g + 1) * lq] - bias_ref[hd]
        if first_valid is not None:
            sg = jnp.where(col_id >= first_valid, sg, -jnp.inf)
        sink = sinks_ref[hd]
        mx = jnp.maximum(jnp.max(sg, axis=-1, keepdims=True), sink)
        p = jnp.exp(sg - mx)
        dens.append(jnp.sum(p, axis=-1, keepdims=True) + jnp.exp(sink - mx))
        ps.append(p.astype(BF16))
    o = _dot(jnp.concatenate(ps, axis=0), vv)
    return [(o[g * lq:(g + 1) * lq] / dens[g]).astype(BF16) for g in range(GROUP)]


def _read_pieces(x_ref):
    return jnp.concatenate([x_ref[0, :, j * D_MODEL:(j + 1) * D_MODEL] for j in range(PHASES)], axis=0)


def _write_pieces(o_ref, out, g_rows):
    for j in range(PHASES):
        o_ref[0, :, j * D_MODEL:(j + 1) * D_MODEL] = out[j * g_rows:(j + 1) * g_rows]


def _mixer_prompt_kernel(x_ref, g1_ref, w_in_ref, bgate_ref, cw_ref, cb_ref, wg_ref, ba_ref, bx_ref, lam_ref,
                         qg_ref, kg_ref, sinks_ref, bias_ref, wr_ref, wa_ref, wo_ref,
                         x1_ref, conv_out_ref, h_out_ref, k_out_ref, v_out_ref,
                         cbuf, ybuf, hcar, kbuf, vbuf, qbuf, yabuf, glbuf, *, ts):
    g_rows = ts // PHASES
    n_chunks = ts // CHUNK
    t = pl.program_id(1)

    @pl.when(t == 0)
    def _():
        cbuf[...] = jnp.zeros_like(cbuf)
        hcar[...] = jnp.zeros_like(hcar)
        kbuf[0:WINDOW, :] = jnp.zeros((WINDOW, KV_W), BF16)
        vbuf[0:WINDOW, :] = jnp.zeros((WINDOW, KV_W), BF16)

    x = _read_pieces(x_ref)
    hb = _rms_rows(x, g1_ref[...]).astype(BF16)

    neg_c_softplus = -RGLRU_C * _softplus_neg(lam_ref[...])
    row_id = lax.broadcasted_iota(jnp.int32, (SUBLANES, MXU_COLS), 0)

    def project_q(half):
        q = _dot(hb, w_in_ref[:, Q_OFF + half * 512:Q_OFF + (half + 1) * 512])
        for j in range(4):
            hd = half * 4 + j
            qn = _rms_rows(q[:, j * HEAD_DIM:(j + 1) * HEAD_DIM], qg_ref[...])
            qbuf[:, hd * HEAD_DIM:(hd + 1) * HEAD_DIM] = _piece_to_chunk_order(qn, g_rows).astype(BF16)

    def project_kv():
        kvp = _dot(hb, w_in_ref[:, K_OFF:K_OFF + 2 * KV_W])
        kn = jnp.concatenate(
            [_rms_rows(kvp[:, j * HEAD_DIM:(j + 1) * HEAD_DIM], kg_ref[...]) for j in range(N_KV_HEADS)], axis=1)
        kn = _piece_to_chunk_order(kn, g_rows)
        v = _piece_to_chunk_order(kvp[:, KV_W:2 * KV_W], g_rows)
        k_out_ref[0] = kn[ts - WINDOW:ts]
        v_out_ref[0] = v[ts - WINDOW:ts]
        kbuf[WINDOW:WINDOW + ts, :] = kn.astype(BF16)
        vbuf[WINDOW:WINDOW + ts, :] = v.astype(BF16)

    def project_gate(part):
        glbuf[:, part * 512:(part + 1) * 512] = _dot(
            hb, w_in_ref[:, G_OFF + part * 512:G_OFF + (part + 1) * 512])

    side_work = [[], [lambda: project_gate(0)], [lambda: project_gate(1)], []]
    tail_work = [lambda: project_gate(2), lambda: project_gate(3)]

    n_blocks = D_RNN // MXU_COLS
    per_block = MXU_COLS // RNN_BLOCK
    hist = RNN_CONV - 1

    def u_proj(m):
        return _dot(hb, w_in_ref[:, U_OFF + m * MXU_COLS:U_OFF + (m + 1) * MXU_COLS])

    def conv_block(m, u):
        cs = slice(m * MXU_COLS, (m + 1) * MXU_COLS)
        shifted = []
        for k in range(hist):
            p = PHASES - hist + k
            piece = u[p * g_rows:(p + 1) * g_rows]
            shifted.append(_shift_rows(piece, cbuf[k, :, cs], row_id))
            cbuf[k, :, cs] = piece[g_rows - SUBLANES:g_rows]
            conv_out_ref[0, k:k + 1, cs] = piece[g_rows - 1:g_rows]
        ext = jnp.concatenate(shifted + [u], axis=0)
        acc = ext[hist * g_rows:hist * g_rows + ts] * cw_ref[hist:hist + 1, cs]
        for k in range(hist):
            acc = acc + ext[k * g_rows:k * g_rows + ts] * cw_ref[k:k + 1, cs]
        return acc + cb_ref[:, cs]

    def gate_dots(m, ucb):
        return [_dot(ucb[:, j * RNN_BLOCK:(j + 1) * RNN_BLOCK], wg_ref[m * per_block + j]) for j in range(per_block)]

    def gate_block(m, zs, uc):
        a_parts, b_parts = [], []
        for j, z in enumerate(zs):
            sl = slice((m * per_block + j) * RNN_BLOCK, (m * per_block + j + 1) * RNN_BLOCK)
            a, b = _gate_math(z, uc[:, j * RNN_BLOCK:(j + 1) * RNN_BLOCK], ba_ref[:, sl], bx_ref[:, sl],
                              neg_c_softplus[:, sl])
            a_parts.append(a)
            b_parts.append(b)
        return jnp.concatenate(a_parts, axis=1), jnp.concatenate(b_parts, axis=1)

    def scan_block(m, a_all, b_all):
        cs = slice(m * MXU_COLS, (m + 1) * MXU_COLS)
        h0 = hcar[:, cs]
        loc = b_all[0:g_rows]
        dec = a_all[0:g_rows]
        locs, decs = [loc], [dec]
        for j in range(1, PHASES):
            aj = a_all[j * g_rows:(j + 1) * g_rows]
            loc = aj * loc + b_all[j * g_rows:(j + 1) * g_rows]
            dec = aj * dec
            locs.append(loc)
            decs.append(dec)
        ends, carry = _scan_rows(dec, loc, h0, row_id)
        entry = _shift_rows(ends, h0, row_id)
        outs = [locs[j] + decs[j] * entry for j in range(PHASES - 1)] + [ends]
        ybuf[:, cs] = jnp.concatenate(outs, axis=0).astype(BF16)
        h_out_ref[0, :, cs] = carry[0:1, :]
        hcar[:, cs] = carry

    lq, lk = CHUNK, WINDOW + CHUNK
    col_id = lax.broadcasted_iota(jnp.int32, (lq, lk), 1)

    def scores(unit):
        c, kv = unit
        kk = kbuf[c * lq:c * lq + lk, kv * HEAD_DIM:(kv + 1) * HEAD_DIM]
        qs = jnp.concatenate(
            [qbuf[c * lq:(c + 1) * lq, (kv * GROUP + g) * HEAD_DIM:(kv * GROUP + g + 1) * HEAD_DIM]
             for g in range(GROUP)], axis=0)
        return _dot_nt(qs, kk)

    def softmax_pv(unit, sc):
        c, kv = unit
        vv = vbuf[c * lq:c * lq + lk, kv * HEAD_DIM:(kv + 1) * HEAD_DIM]
        first_valid = jnp.where(t == 0, WINDOW - c * lq, 0) if c * lq < WINDOW else None
        outs = _attend(sc, vv, bias_ref, sinks_ref, kv, lq, first_valid, col_id)
        for g in range(GROUP):
            hd = kv * GROUP + g
            yabuf[c * lq:(c + 1) * lq, hd * HEAD_DIM:(hd + 1) * HEAD_DIM] = outs[g]

    units = [(c, kv) for c in range(n_chunks) for kv in range(N_KV_HEADS)]
    attn_work = [units[(len(units) * i) // n_blocks:(len(units) * (i + 1)) // n_blocks] for i in range(n_blocks)]
    project_q(0)
    project_q(1)
    uc = conv_block(0, u_proj(0))
    project_kv()
    for m in range(n_blocks):
        zs = gate_dots(m, uc.astype(BF16))
        u_next = u_proj(m + 1) if m + 1 < n_blocks else None
        for work in side_work[m]:
            work()
        scs = [scores(unit) for unit in attn_work[m]]
        a_all, b_all = gate_block(m, zs, uc)
        if u_next is not None:
            uc = conv_block(m + 1, u_next)
        scan_block(m, a_all, b_all)
        if m + 1 == n_blocks:
            for work in tail_work:
                work()
        for unit, sc in zip(attn_work[m], scs):
            softmax_pv(unit, sc)
    kbuf[0:WINDOW, :] = kbuf[ts:ts + WINDOW, :]
    vbuf[0:WINDOW, :] = vbuf[ts:ts + WINDOW, :]

    acc = None
    for p in range(D_MODEL // MXU_COLS):
        ps = slice(p * MXU_COLS, (p + 1) * MXU_COLS)
        gs = slice(D_MODEL + p * MXU_COLS, D_MODEL + (p + 1) * MXU_COLS)
        rnn_proj = _dot(ybuf[...], wr_ref[:, ps])
        attn_proj = _chunk_to_piece_order(_dot(yabuf[...], wa_ref[:, ps]), g_rows)
        g_rnn = _sigmoid(glbuf[:, ps] + bgate_ref[:, ps])
        g_attn = _sigmoid(glbuf[:, gs] + bgate_ref[:, gs])
        d = _dot((g_rnn * rnn_proj + g_attn * attn_proj).astype(BF16), wo_ref[ps, :])
        acc = d if acc is None else acc + d
    _write_pieces(x1_ref, x + acc, g_rows)


def _ffn_prompt_kernel(x_ref, g2_ref, wup_ref, cw_ref, cb_ref, wdn_ref, y_ref, conv_out_ref, cbuf, hbuf, *, ts):
    g_rows = ts // PHASES
    hist = FFN_CONV - 1
    t = pl.program_id(1)

    @pl.when(t == 0)
    def _():
        cbuf[...] = jnp.zeros_like(cbuf)

    x = _read_pieces(x_ref)
    hb = _rms_rows(x, g2_ref[...]).astype(BF16)
    row_id = lax.broadcasted_iota(jnp.int32, (SUBLANES, MXU_COLS), 0)

    for m in range(D_FF // MXU_COLS):
        cs = slice(m * MXU_COLS, (m + 1) * MXU_COLS)
        a = _dot(hb, wup_ref[:, m * MXU_COLS:(m + 1) * MXU_COLS])
        b = _dot(hb, wup_ref[:, D_FF + m * MXU_COLS:D_FF + (m + 1) * MXU_COLS])
        shifted = []
        for k in range(hist):
            p = PHASES - hist + k
            piece = a[p * g_rows:(p + 1) * g_rows]
            shifted.append(_shift_rows(piece, cbuf[k, :, cs], row_id))
            cbuf[k, :, cs] = piece[g_rows - SUBLANES:g_rows]
            conv_out_ref[0, k:k + 1, cs] = piece[g_rows - 1:g_rows]
        ext = jnp.concatenate(shifted + [a], axis=0)
        acc = ext[hist * g_rows:hist * g_rows + ts] * cw_ref[hist:hist + 1, cs]
        for k in range(hist):
            acc = acc + ext[k * g_rows:k * g_rows + ts] * cw_ref[k:k + 1, cs]
        ac = acc + cb_ref[:, cs]
        hbuf[:, cs] = (_gelu_exact(ac) * b).astype(BF16)
    _write_pieces(y_ref, x + _dot(hbuf[...], wdn_ref[...]), g_rows)


def _mixer_sample_kernel(x_ref, sconv_ref, sh_ref, kc_ref, vc_ref,
                         g1_ref, w_in_ref, bgate_ref, cw_ref, cb_ref, wg_ref, ba_ref, bx_ref, lam_ref, qg_ref, kg_ref,
                         sinks_ref, bias_ref, wr_ref, wa_ref, wo_ref,
                         x1_ref, conv_out_ref, h_out_ref, k_out_ref, v_out_ref,
                         ubuf, kbuf, vbuf, *, nb, ts):
    rows = nb * ts
    hist = RNN_CONV - 1
    x = x_ref[...].reshape(rows, D_MODEL)
    hb = _rms_rows(x, g1_ref[...]).astype(BF16)
    neg_c_softplus = -RGLRU_C * _softplus_neg(lam_ref[...])
    row_id = lax.broadcasted_iota(jnp.int32, (SUBLANES, D_RNN), 0)

    u = _dot(hb, w_in_ref[:, U_OFF:U_OFF + D_RNN])
    ucs = []
    for s in range(nb):
        ubuf[s, HIST - hist:HIST, :] = sconv_ref[s]
        ubuf[s, HIST:HIST + ts, :] = u[s * ts:(s + 1) * ts]
        acc = u[s * ts:(s + 1) * ts] * cw_ref[hist:hist + 1, :]
        for j in range(hist):
            acc = acc + ubuf[s, HIST - hist + j:HIST - hist + j + ts, :] * cw_ref[j:j + 1, :]
        ucs.append(acc + cb_ref[...])
        conv_out_ref[s] = ubuf[s, HIST + ts - hist:HIST + ts, :]
    uc = jnp.concatenate(ucs, axis=0)
    ucb = uc.astype(BF16)
    a_parts, b_parts = [], []
    for n in range(N_RNN_BLOCKS):
        sl = slice(n * RNN_BLOCK, (n + 1) * RNN_BLOCK)
        a, b = _gate_math(_dot(ucb[:, sl], wg_ref[n]), uc[:, sl], ba_ref[:, sl], bx_ref[:, sl], neg_c_softplus[:, sl])
        a_parts.append(a)
        b_parts.append(b)
    a_all = jnp.concatenate(a_parts, axis=1)
    b_all = jnp.concatenate(b_parts, axis=1)
    ys = []
    for s in range(nb):
        h0 = jnp.broadcast_to(sh_ref[s], (SUBLANES, D_RNN))
        y, hl = _scan_rows(a_all[s * ts:(s + 1) * ts], b_all[s * ts:(s + 1) * ts], h0, row_id)
        ys.append(y)
        h_out_ref[s] = hl[0:1, :]
    g_rnn = _sigmoid(_dot(hb, w_in_ref[:, G_OFF:G_OFF + D_MODEL]) + bgate_ref[:, 0:D_MODEL])
    mix_rnn = g_rnn * _dot(jnp.concatenate(ys, axis=0).astype(BF16), wr_ref[...])

    q = _dot(hb, w_in_ref[:, Q_OFF:Q_OFF + Q_W])
    qn = jnp.concatenate([_rms_rows(q[:, hd * HEAD_DIM:(hd + 1) * HEAD_DIM], qg_ref[...])
                          for hd in range(N_HEADS)], axis=1).astype(BF16)
    kvp = _dot(hb, w_in_ref[:, K_OFF:K_OFF + 2 * KV_W])
    kn = jnp.concatenate(
        [_rms_rows(kvp[:, j * HEAD_DIM:(j + 1) * HEAD_DIM], kg_ref[...]) for j in range(N_KV_HEADS)], axis=1)
    v = kvp[:, KV_W:2 * KV_W]
    lq, lk = ts, WINDOW + ts
    attn_rows = []
    for s in range(nb):
        kbuf[s, 0:WINDOW, :] = kc_ref[s].astype(BF16)
        vbuf[s, 0:WINDOW, :] = vc_ref[s].astype(BF16)
        kbuf[s, WINDOW:lk, :] = kn[s * ts:(s + 1) * ts].astype(BF16)
        vbuf[s, WINDOW:lk, :] = v[s * ts:(s + 1) * ts].astype(BF16)
        k_out_ref[s, 0:WINDOW - ts, :] = kc_ref[s, ts:WINDOW, :]
        v_out_ref[s, 0:WINDOW - ts, :] = vc_ref[s, ts:WINDOW, :]
        k_out_ref[s, WINDOW - ts:WINDOW, :] = kn[s * ts:(s + 1) * ts]
        v_out_ref[s, WINDOW - ts:WINDOW, :] = v[s * ts:(s + 1) * ts]
        heads = []
        for kv in range(N_KV_HEADS):
            ksl = slice(kv * HEAD_DIM, (kv + 1) * HEAD_DIM)
            qs = jnp.concatenate(
                [qn[s * ts:(s + 1) * ts, (kv * GROUP + g) * HEAD_DIM:(kv * GROUP + g + 1) * HEAD_DIM]
                 for g in range(GROUP)], axis=0)
            heads += _attend(_dot_nt(qs, kbuf[s, :, ksl]), vbuf[s, :, ksl], bias_ref, sinks_ref, kv, lq, None, None)
        attn_rows.append(jnp.concatenate(heads, axis=1))
    y_attn = jnp.concatenate(attn_rows, axis=0)

    g_attn = _sigmoid(_dot(hb, w_in_ref[:, G_OFF + D_MODEL:G_OFF + 2 * D_MODEL]) + bgate_ref[:, D_MODEL:2 * D_MODEL])
    mixed = mix_rnn + g_attn * _dot(y_attn, wa_ref[...])
    out = x + _dot(mixed.astype(BF16), wo_ref[...])
    x1_ref[...] = out.reshape(nb, ts, D_MODEL)


def _ffn_sample_kernel(x_ref, sconv_ref, g2_ref, wup_ref, cw_ref, cb_ref, wdn_ref, y_ref, conv_out_ref, abuf,
                       *, nb, ts):
    rows = nb * ts
    hist = FFN_CONV - 1
    x = x_ref[...].reshape(rows, D_MODEL)
    hb = _rms_rows(x, g2_ref[...]).astype(BF16)
    a = _dot(hb, wup_ref[:, 0:D_FF])
    acs = []
    for s in range(nb):
        abuf[s, HIST - hist:HIST, :] = sconv_ref[s]
        abuf[s, HIST:HIST + ts, :] = a[s * ts:(s + 1) * ts]
        acc = a[s * ts:(s + 1) * ts] * cw_ref[hist:hist + 1, :]
        for j in range(hist):
            acc = acc + abuf[s, HIST - hist + j:HIST - hist + j + ts, :] * cw_ref[j:j + 1, :]
        acs.append(acc + cb_ref[...])
        conv_out_ref[s] = abuf[s, HIST + ts - hist:HIST + ts, :]
    ac = jnp.concatenate(acs, axis=0)
    b = _dot(hb, wup_ref[:, D_FF:2 * D_FF])
    out = x + _dot((_gelu_exact(ac) * b).astype(BF16), wdn_ref[...])
    y_ref[...] = out.reshape(nb, ts, D_MODEL)


def _const_spec(shape):
    nd = len(shape)
    return pl.BlockSpec(shape, lambda b, t: (0,) * nd, pipeline_mode=pl.Buffered(1))


def _alibi_bias(q_time, k_time):
    slopes = np.array([2.0 ** (-8.0 * (h + 1) / N_HEADS) for h in range(N_HEADS)], dtype=np.float32)
    dist = np.abs(q_time[:, None] - k_time[None, :]).astype(np.float32)
    return jnp.asarray(slopes[:, None, None] * dist[None])


def _chunk_row_time():
    r = np.arange(CHUNK)
    return (r % SUBLANES) * PHASES + r // SUBLANES


def _from_chunk_order(a):
    bsz, n, c = a.shape
    a = a.reshape(bsz, n // CHUNK, PHASES, SUBLANES, c)
    return jnp.swapaxes(a, 2, 3).reshape(bsz, n, c)


def _mixer_weight_args(w):
    head = [w['g1'], w['w_in'], w['b_gate'], w['rnn_conv_w'], w['rnn_conv_b'], w['w_gate'], w['b_a'], w['b_x'],
            w['lam'], w['q_g'], w['k_g']]
    tail = [w['w_rnn_proj'], w['w_attn_proj'], w['w_out']]
    return head, tail


def _params():
    return pltpu.CompilerParams(dimension_semantics=('arbitrary', 'arbitrary'), vmem_limit_bytes=VMEM_LIMIT_BYTES)


def _mixer_state_shapes(bsz):
    return (jax.ShapeDtypeStruct((bsz, RNN_CONV - 1, D_RNN), F32),
            jax.ShapeDtypeStruct((bsz, 1, D_RNN), F32),
            jax.ShapeDtypeStruct((bsz, WINDOW, KV_W), F32),
            jax.ShapeDtypeStruct((bsz, WINDOW, KV_W), F32))


def _mixer_state_specs(nb):
    per_b = lambda b, t: (b, 0, 0)
    return (pl.BlockSpec((nb, RNN_CONV - 1, D_RNN), per_b), pl.BlockSpec((nb, 1, D_RNN), per_b),
            pl.BlockSpec((nb, WINDOW, KV_W), per_b), pl.BlockSpec((nb, WINDOW, KV_W), per_b))


def _mixer_prompt_call(xr, w, *, ts):
    bsz, n_seg, width = xr.shape
    g_rows = ts // PHASES
    tile = pl.BlockSpec((1, g_rows, width), lambda b, t: (b, t, 0))
    head, tail = _mixer_weight_args(w)
    chunk_time = _chunk_row_time()
    bias = _alibi_bias(WINDOW + chunk_time,
                       np.concatenate([d * CHUNK + chunk_time for d in range(WINDOW // CHUNK + 1)]))
    in_specs = ([tile] + [_const_spec(a.shape) for a in head]
                + [pl.BlockSpec(memory_space=pltpu.SMEM), _const_spec(bias.shape)]
                + [_const_spec(a.shape) for a in tail])
    scratch = [pltpu.VMEM((RNN_CONV - 1, SUBLANES, D_RNN), F32),
               pltpu.VMEM((ts, D_RNN), BF16),
               pltpu.VMEM((SUBLANES, D_RNN), F32),
               pltpu.VMEM((WINDOW + ts, KV_W), BF16),
               pltpu.VMEM((WINDOW + ts, KV_W), BF16),
               pltpu.VMEM((ts, Q_W), BF16),
               pltpu.VMEM((ts, Q_W), BF16),
               pltpu.VMEM((ts, 2 * D_MODEL), F32)]
    x1, conv, h, k, v = pl.pallas_call(
        functools.partial(_mixer_prompt_kernel, ts=ts),
        out_shape=(jax.ShapeDtypeStruct(xr.shape, F32),) + _mixer_state_shapes(bsz),
        grid=(bsz, n_seg // g_rows), in_specs=in_specs, out_specs=(tile,) + _mixer_state_specs(1),
        scratch_shapes=scratch, name='mixer_prompt', compiler_params=_params(),
    )(xr, *head, w['sinks'], bias, *tail)
    return x1, conv, h, _from_chunk_order(k), _from_chunk_order(v)


def _ffn_weight_args(w):
    return [w['g2'], w['w_up'], w['ffn_conv_w'], w['ffn_conv_b'], w['w_down']]


def _ffn_prompt_call(xr, w, *, ts):
    bsz, n_seg, width = xr.shape
    g_rows = ts // PHASES
    tile = pl.BlockSpec((1, g_rows, width), lambda b, t: (b, t, 0))
    weights = _ffn_weight_args(w)
    state_spec = pl.BlockSpec((1, FFN_CONV - 1, D_FF), lambda b, t: (b, 0, 0))
    scratch = [pltpu.VMEM((FFN_CONV - 1, SUBLANES, D_FF), F32),
               pltpu.VMEM((ts, D_FF), BF16)]
    return pl.pallas_call(
        functools.partial(_ffn_prompt_kernel, ts=ts),
        out_shape=(jax.ShapeDtypeStruct(xr.shape, F32), jax.ShapeDtypeStruct((bsz, FFN_CONV - 1, D_FF), F32)),
        grid=(bsz, n_seg // g_rows), in_specs=[tile] + [_const_spec(a.shape) for a in weights],
        out_specs=(tile, state_spec), scratch_shapes=scratch, name='ffn_prompt', compiler_params=_params(),
    )(xr, *weights)


def _mixer_sample_call(x, state, w):
    nb, ts, _ = x.shape
    sconv, sh, kc, vc = state
    whole = pl.BlockSpec((nb, ts, D_MODEL), lambda b, t: (0, 0, 0))
    head, tail = _mixer_weight_args(w)
    bias = _alibi_bias(WINDOW + np.arange(ts), np.arange(WINDOW + ts))
    in_specs = ([whole] + list(_mixer_state_specs(nb)) + [_const_spec(a.shape) for a in head]
                + [pl.BlockSpec(memory_space=pltpu.SMEM), _const_spec(bias.shape)]
                + [_const_spec(a.shape) for a in tail])
    scratch = [pltpu.VMEM((nb, HIST + ts, D_RNN), F32),
               pltpu.VMEM((nb, WINDOW + ts, KV_W), BF16),
               pltpu.VMEM((nb, WINDOW + ts, KV_W), BF16)]
    return pl.pallas_call(
        functools.partial(_mixer_sample_kernel, nb=nb, ts=ts),
        out_shape=(jax.ShapeDtypeStruct(x.shape, F32),) + _mixer_state_shapes(nb),
        grid=(1, 1), in_specs=in_specs, out_specs=(whole,) + _mixer_state_specs(nb),
        scratch_shapes=scratch, name='mixer_sample', compiler_params=_params(),
    )(x, sconv, sh.reshape(nb, 1, D_RNN), kc.reshape(nb, WINDOW, KV_W), vc.reshape(nb, WINDOW, KV_W),
      *head, w['sinks'], bias, *tail)


def _ffn_sample_call(x, sconv, w):
    nb, ts, _ = x.shape
    whole = pl.BlockSpec((nb, ts, D_MODEL), lambda b, t: (0, 0, 0))
    state_spec = pl.BlockSpec((nb, FFN_CONV - 1, D_FF), lambda b, t: (0, 0, 0))
    weights = _ffn_weight_args(w)
    return pl.pallas_call(
        functools.partial(_ffn_sample_kernel, nb=nb, ts=ts),
        out_shape=(jax.ShapeDtypeStruct(x.shape, F32), jax.ShapeDtypeStruct((nb, FFN_CONV - 1, D_FF), F32)),
        grid=(1, 1), in_specs=[whole, state_spec] + [_const_spec(a.shape) for a in weights],
        out_specs=(whole, state_spec), scratch_shapes=[pltpu.VMEM((nb, HIST + ts, D_FF), F32)],
        name='ffn_sample', compiler_params=_params(),
    )(x, sconv, *weights)


def kernel(x_prompt, x_sample, state_rnn_conv, state_rnn_h, cache_attn_k, cache_attn_v, state_ffn_conv, norm_mix_g, w_in, b_gate, rnn_conv_w, rnn_conv_b, rnn_gate_a_w, rnn_gate_a_b, rnn_gate_x_w, rnn_gate_x_b, rnn_lambda, q_norm_g, k_norm_g, attn_sinks, w_rnn_proj, w_attn_proj, w_out, norm_ffn_g, w_up, ffn_conv_w, ffn_conv_b, w_down):
    depth = w_in.shape[0]
    bsz, seq, _ = x_prompt.shape
    xp = x_prompt.reshape(bsz, seq // PHASES, PHASES * D_MODEL)
    xs = x_sample
    outs = [[] for _ in range(10)]
    for l in range(depth):
        w = {
            'g1': norm_mix_g[l][None], 'w_in': w_in[l].astype(BF16), 'b_gate': b_gate[l][None],
            'rnn_conv_w': rnn_conv_w[l], 'rnn_conv_b': rnn_conv_b[l][None],
            'w_gate': jnp.concatenate([rnn_gate_a_w[l], rnn_gate_x_w[l]], axis=-1).astype(BF16),
            'b_a': rnn_gate_a_b[l][None], 'b_x': rnn_gate_x_b[l][None], 'lam': rnn_lambda[l][None],
            'q_g': q_norm_g[l][None], 'k_g': k_norm_g[l][None], 'sinks': attn_sinks[l],
            'w_rnn_proj': w_rnn_proj[l].astype(BF16), 'w_attn_proj': w_attn_proj[l].astype(BF16),
            'w_out': w_out[l].astype(BF16),
            'g2': norm_ffn_g[l][None], 'w_up': w_up[l].astype(BF16), 'ffn_conv_w': ffn_conv_w[l],
            'ffn_conv_b': ffn_conv_b[l][None], 'w_down': w_down[l].astype(BF16),
        }
        xp, conv_p, h_p, k_p, v_p = _mixer_prompt_call(xp, w, ts=PROMPT_TILE)
        xp, fconv_p = _ffn_prompt_call(xp, w, ts=PROMPT_TILE)
        xs, conv_s, h_s, k_s, v_s = _mixer_sample_call(
            xs, (state_rnn_conv[l], state_rnn_h[l], cache_attn_k[l], cache_attn_v[l]), w)
        xs, fconv_s = _ffn_sample_call(xs, state_ffn_conv[l], w)
        kv_shape = lambda a: a.reshape(a.shape[0], WINDOW, N_KV_HEADS, HEAD_DIM)
        for j, o in enumerate((conv_p, conv_s, h_p[:, 0], h_s[:, 0], kv_shape(k_p), kv_shape(k_s),
                               kv_shape(v_p), kv_shape(v_s), fconv_p, fconv_s)):
            outs[j].append(o)
    return (xp.reshape(bsz, seq, D_MODEL), xs) + tuple(jnp.stack(o) for o in outs)
```

```python
import functools
import math

import jax
import jax.numpy as jnp
import numpy as np
from jax import lax
from jax.experimental import pallas as pl
from jax.experimental.pallas import tpu as pltpu

D_MODEL = 1024
CHUNK = 64
WINDOW = 128
N_HEADS = 8
N_KV_HEADS = 2
HEAD_DIM = 128
GROUP = N_HEADS // N_KV_HEADS
ATTN_SCALE = HEAD_DIM ** -0.5
D_RNN = D_MODEL
N_RNN_BLOCKS = 8
RNN_BLOCK = D_RNN // N_RNN_BLOCKS
RNN_CONV = 4
RGLRU_C = 8.0
D_FF = ((8 * D_MODEL // 3 + 127) // 128) * 128
FFN_CONV = 3
EPS = 1e-6
Q_W = N_HEADS * HEAD_DIM
KV_W = N_KV_HEADS * HEAD_DIM
IN_W = D_RNN + Q_W + 2 * KV_W + 2 * D_MODEL
U_OFF, Q_OFF, K_OFF, V_OFF, G_OFF = 0, D_RNN, D_RNN + Q_W, D_RNN + Q_W + KV_W, D_RNN + Q_W + 2 * KV_W

SUBLANES = 8
MXU_COLS = 256
HIST = SUBLANES
VMEM_LIMIT_BYTES = 56 * 1024 * 1024
LOG2E = 1.4426950408889634
NEG_LOG2E = -LOG2E

F32 = jnp.float32
BF16 = jnp.bfloat16


def _sigmoid(x):
    return 1.0 / (1.0 + jnp.exp2(x * NEG_LOG2E))


def _sqrt_nonneg(y):
    return jnp.where(y > 0.0, y * lax.rsqrt(y), 0.0)


def _rms_rows(x, g):
    r = lax.rsqrt(jnp.mean(x * x, axis=-1, keepdims=True) + EPS)
    return x * r * g


def _dot(a, b):
    return jnp.dot(a, b, preferred_element_type=F32)


def _dot_nt(a, b):
    return lax.dot_general(a, b, (((1,), (1,)), ((), ())), preferred_element_type=F32)


def _scan_rows(a, b, h0, row_id):
    outs = []
    carry = h0
    for g in range(a.shape[0] // SUBLANES):
        ag = a[g * SUBLANES:(g + 1) * SUBLANES]
        bg = b[g * SUBLANES:(g + 1) * SUBLANES]
        for sh in (1, 2, 4):
            m = row_id >= sh
            bg = jnp.where(m, ag * pltpu.roll(bg, sh, 0) + bg, bg)
            ag = jnp.where(m, ag * pltpu.roll(ag, sh, 0), ag)
        h = ag * carry + bg
        outs.append(h)
        carry = jnp.broadcast_to(h[SUBLANES - 1:SUBLANES, :], h.shape)
    return (outs[0] if len(outs) == 1 else jnp.concatenate(outs, axis=0)), carry


def _mixer_kernel(*refs, nb, ts, lq, n_chunks, sample):
    if sample:
        (x_ref, sconv_ref, sh_ref, kc_ref, vc_ref, *rest) = refs
    else:
        (x_ref, *rest) = refs
    (g1_ref, w_in_ref, bgate_ref, cw_ref, cb_ref, wg_ref, ba_ref, bx_ref, lam_ref, qg_ref, kg_ref,
     sinks_ref, bias_ref, wr_ref, wa_ref, wo_ref,
     x1_ref, conv_out_ref, h_out_ref, k_out_ref, v_out_ref,
     ubuf, ybuf, hcar, kbuf, vbuf, qbuf, yabuf, glbuf, mixbuf) = rest

    rows = nb * ts
    t = pl.program_id(1)

    if not sample:
        @pl.when(t == 0)
        def _():
            ubuf[:, 0:HIST, :] = jnp.zeros((nb, HIST, D_RNN), F32)
            hcar[...] = jnp.zeros_like(hcar)
            kbuf[:, 0:WINDOW, :] = jnp.zeros((nb, WINDOW, KV_W), BF16)
            vbuf[:, 0:WINDOW, :] = jnp.zeros((nb, WINDOW, KV_W), BF16)

    x = x_ref[...].reshape(rows, D_MODEL)
    hb = _rms_rows(x, g1_ref[...]).astype(BF16)

    lam = lam_ref[...]
    neg_c_softplus = -RGLRU_C * (jnp.maximum(-lam, 0.0) + jnp.log1p(jnp.exp(-jnp.abs(lam))))
    row_id = lax.broadcasted_iota(jnp.int32, (SUBLANES, MXU_COLS), 0)

    def project_q(half):
        q = _dot(hb, w_in_ref[:, Q_OFF + half * 512:Q_OFF + (half + 1) * 512])
        for j in range(4):
            hd = half * 4 + j
            qbuf[:, hd * HEAD_DIM:(hd + 1) * HEAD_DIM] = _rms_rows(
                q[:, j * HEAD_DIM:(j + 1) * HEAD_DIM], qg_ref[...]).astype(BF16)

    def project_kv():
        kvp = _dot(hb, w_in_ref[:, K_OFF:K_OFF + 2 * KV_W])
        kn = jnp.concatenate(
            [_rms_rows(kvp[:, j * HEAD_DIM:(j + 1) * HEAD_DIM], kg_ref[...]) for j in range(N_KV_HEADS)], axis=1)
        v = kvp[:, KV_W:2 * KV_W]
        for s in range(nb):
            if sample:
                kbuf[s, 0:WINDOW, :] = kc_ref[s].astype(BF16)
                vbuf[s, 0:WINDOW, :] = vc_ref[s].astype(BF16)
                k_out_ref[s, 0:WINDOW - ts, :] = kc_ref[s, ts:WINDOW, :]
                v_out_ref[s, 0:WINDOW - ts, :] = vc_ref[s, ts:WINDOW, :]
                k_out_ref[s, WINDOW - ts:WINDOW, :] = kn[s * ts:(s + 1) * ts]
                v_out_ref[s, WINDOW - ts:WINDOW, :] = v[s * ts:(s + 1) * ts]
            else:
                k_out_ref[s] = kn[(s + 1) * ts - WINDOW:(s + 1) * ts]
                v_out_ref[s] = v[(s + 1) * ts - WINDOW:(s + 1) * ts]
            kbuf[s, WINDOW:WINDOW + ts, :] = kn[s * ts:(s + 1) * ts].astype(BF16)
            vbuf[s, WINDOW:WINDOW + ts, :] = v[s * ts:(s + 1) * ts].astype(BF16)

    def project_gate(part):
        glbuf[:, part * 512:(part + 1) * 512] = _dot(
            hb, w_in_ref[:, G_OFF + part * 512:G_OFF + (part + 1) * 512])

    side_work = [[lambda: project_q(1)],
                 [project_kv, lambda: project_gate(0)],
                 [lambda: project_gate(1)],
                 [lambda: project_gate(2)]]
    tail_work = [lambda: project_gate(3)]

    n_blocks = D_RNN // MXU_COLS
    per_block = MXU_COLS // RNN_BLOCK

    def u_proj(m):
        return _dot(hb, w_in_ref[:, U_OFF + m * MXU_COLS:U_OFF + (m + 1) * MXU_COLS])

    def conv_block(m, u):
        cs = slice(m * MXU_COLS, (m + 1) * MXU_COLS)
        ucs = []
        for s in range(nb):
            if sample:
                ubuf[s, HIST - (RNN_CONV - 1):HIST, cs] = sconv_ref[s, :, cs]
            ubuf[s, HIST:HIST + ts, cs] = u[s * ts:(s + 1) * ts]
            acc = u[s * ts:(s + 1) * ts] * cw_ref[RNN_CONV - 1:RNN_CONV, cs]
            for j in range(RNN_CONV - 1):
                off = HIST - (RNN_CONV - 1) + j
                acc = acc + ubuf[s, off:off + ts, cs] * cw_ref[j:j + 1, cs]
            ucs.append(acc + cb_ref[:, cs])
            conv_out_ref[s, :, cs] = ubuf[s, HIST + ts - (RNN_CONV - 1):HIST + ts, cs]
            if not sample:
                ubuf[s, 0:HIST, cs] = ubuf[s, ts:ts + HIST, cs]
        return ucs[0] if nb == 1 else jnp.concatenate(ucs, axis=0)

    def gate_dots(m, ucb):
        return [_dot(ucb[:, j * RNN_BLOCK:(j + 1) * RNN_BLOCK], wg_ref[m * per_block + j]) for j in range(per_block)]

    def gate_math(m, zs, uc):
        a_parts, b_parts = [], []
        for j, z in enumerate(zs):
            n = m * per_block + j
            sl = slice(n * RNN_BLOCK, (n + 1) * RNN_BLOCK)
            r = _sigmoid(z[:, :RNN_BLOCK] + ba_ref[:, sl])
            i = _sigmoid(z[:, RNN_BLOCK:] + bx_ref[:, sl])
            a = jnp.exp(neg_c_softplus[:, sl] * r)
            a_parts.append(a)
            b_parts.append(_sqrt_nonneg(1.0 - a * a) * (i * uc[:, j * RNN_BLOCK:(j + 1) * RNN_BLOCK]))
        return jnp.concatenate(a_parts, axis=1), jnp.concatenate(b_parts, axis=1)

    def scan_block(m, a_all, b_all):
        cs = slice(m * MXU_COLS, (m + 1) * MXU_COLS)
        for s in range(nb):
            if sample:
                h0 = jnp.broadcast_to(sh_ref[s, :, cs], (SUBLANES, MXU_COLS))
            else:
                h0 = hcar[s, :, cs]
            y, hl = _scan_rows(a_all[s * ts:(s + 1) * ts], b_all[s * ts:(s + 1) * ts], h0, row_id)
            ybuf[s * ts:(s + 1) * ts, cs] = y.astype(BF16)
            h_out_ref[s, :, cs] = hl[0:1, :]
            if not sample:
                hcar[s, :, cs] = hl

    u_first = u_proj(0)
    project_q(0)
    uc = conv_block(0, u_first)
    for m in range(n_blocks):
        zs = gate_dots(m, uc.astype(BF16))
        u_next = u_proj(m + 1) if m + 1 < n_blocks else None
        for work in side_work[m]:
            work()
        a_all, b_all = gate_math(m, zs, uc)
        if u_next is not None:
            uc = conv_block(m + 1, u_next)
        else:
            for work in tail_work:
                work()
        scan_block(m, a_all, b_all)

    lk = WINDOW + lq
    col_id = lax.broadcasted_iota(jnp.int32, (lq, lk), 1)
    units = [(s, c, kv) for s in range(nb) for c in range(n_chunks) for kv in range(N_KV_HEADS)]
    n_pieces = D_MODEL // MXU_COLS
    wr_at = {(len(units) * p) // n_pieces: p for p in range(n_pieces)}

    def rnn_proj(p):
        ps = slice(p * MXU_COLS, (p + 1) * MXU_COLS)
        g_rnn = _sigmoid(glbuf[:, ps] + bgate_ref[:, ps])
        mixbuf[:, ps] = g_rnn * _dot(ybuf[...], wr_ref[:, ps])

    def scores(unit):
        s, c, kv = unit
        r0 = s * ts + c * lq
        kk = kbuf[s, c * lq:c * lq + lk, kv * HEAD_DIM:(kv + 1) * HEAD_DIM]
        qs = jnp.concatenate(
            [qbuf[r0:r0 + lq, (kv * GROUP + g) * HEAD_DIM:(kv * GROUP + g + 1) * HEAD_DIM]
             for g in range(GROUP)], axis=0)
        return _dot_nt(qs, kk)

    def softmax_pv(unit, sc):
        s, c, kv = unit
        r0 = s * ts + c * lq
        vv = vbuf[s, c * lq:c * lq + lk, kv * HEAD_DIM:(kv + 1) * HEAD_DIM]
        sc = sc * (ATTN_SCALE * LOG2E)
        ps = []
        dens = []
        for g in range(GROUP):
            hd = kv * GROUP + g
            sg = sc[g * lq:(g + 1) * lq] - bias_ref[hd]
            if (not sample) and c * lq < WINDOW:
                first_valid = jnp.where(t == 0, WINDOW - c * lq, 0)
                sg = jnp.where(col_id >= first_valid, sg, -jnp.inf)
            sink = sinks_ref[hd] * LOG2E
            mx = jnp.maximum(jnp.max(sg, axis=-1, keepdims=True), sink)
            p = jnp.exp2(sg - mx)
            dens.append(jnp.sum(p, axis=-1, keepdims=True) + jnp.exp2(sink - mx))
            ps.append(p.astype(BF16))
        o = _dot(jnp.concatenate(ps, axis=0), vv)
        for g in range(GROUP):
            hd = kv * GROUP + g
            yabuf[r0:r0 + lq, hd * HEAD_DIM:(hd + 1) * HEAD_DIM] = (o[g * lq:(g + 1) * lq] / dens[g]).astype(BF16)

    sc = scores(units[0])
    for idx, unit in enumerate(units):
        sc_next = scores(units[idx + 1]) if idx + 1 < len(units) else None
        if idx in wr_at:
            rnn_proj(wr_at[idx])
        softmax_pv(unit, sc)
        sc = sc_next
    if not sample:
        for s in range(nb):
            kbuf[s, 0:WINDOW, :] = kbuf[s, ts:ts + WINDOW, :]
            vbuf[s, 0:WINDOW, :] = vbuf[s, ts:ts + WINDOW, :]

    attn_proj = [_dot(yabuf[...], wa_ref[:, p * MXU_COLS:(p + 1) * MXU_COLS]) for p in range(n_pieces)]
    acc = None
    for p in range(n_pieces):
        ps = slice(p * MXU_COLS, (p + 1) * MXU_COLS)
        gs = slice(D_MODEL + p * MXU_COLS, D_MODEL + (p + 1) * MXU_COLS)
        g_attn = _sigmoid(glbuf[:, gs] + bgate_ref[:, gs])
        mixed = mixbuf[:, ps] + g_attn * attn_proj[p]
        d = _dot(mixed.astype(BF16), wo_ref[ps, :])
        acc = d if acc is None else acc + d
    x1_ref[...] = (x + acc).reshape(nb, ts, D_MODEL)


def _ffn_kernel(*refs, nb, ts, sample):
    if sample:
        (x_ref, sconv_ref, *rest) = refs
    else:
        (x_ref, *rest) = refs
    (g2_ref, wup_ref, cw_ref, cb_ref, wdn_ref, y_ref, conv_out_ref, abuf, hbuf) = rest
    rows = nb * ts
    t = pl.program_id(1)

    if not sample:
        @pl.when(t == 0)
        def _():
            abuf[:, 0:HIST, :] = jnp.zeros((nb, HIST, D_FF), F32)

    x = x_ref[...].reshape(rows, D_MODEL)
    hb = _rms_rows(x, g2_ref[...]).astype(BF16)
    a = _dot(hb, wup_ref[:, 0:D_FF])
    for s in range(nb):
        if sample:
            abuf[s, HIST - (FFN_CONV - 1):HIST, :] = sconv_ref[s]
        abuf[s, HIST:HIST + ts, :] = a[s * ts:(s + 1) * ts]
    acs = []
    for s in range(nb):
        acc = abuf[s, HIST:HIST + ts, :] * cw_ref[FFN_CONV - 1:FFN_CONV, :]
        for j in range(FFN_CONV - 1):
            off = HIST - (FFN_CONV - 1) + j
            acc = acc + abuf[s, off:off + ts, :] * cw_ref[j:j + 1, :]
        acs.append(acc + cb_ref[...])
        conv_out_ref[s] = abuf[s, HIST + ts - (FFN_CONV - 1):HIST + ts, :]
        if not sample:
            abuf[s, 0:HIST, :] = abuf[s, ts:ts + HIST, :]
    ac = acs[0] if nb == 1 else jnp.concatenate(acs, axis=0)
    b = _dot(hb, wup_ref[:, D_FF:2 * D_FF])
    gelu = 0.5 * ac * (1.0 + lax.erf(ac * np.float32(math.sqrt(0.5))))
    hbuf[...] = (gelu * b).astype(BF16)
    out = x + _dot(hbuf[...], wdn_ref[...])
    y_ref[...] = out.reshape(nb, ts, D_MODEL)


def _const_spec(shape):
    nd = len(shape)
    return pl.BlockSpec(shape, lambda b, t: (0,) * nd, pipeline_mode=pl.Buffered(1))


def _alibi_bias(lq):
    slopes = np.array([2.0 ** (-8.0 * (h + 1) / N_HEADS) for h in range(N_HEADS)], dtype=np.float32)
    i = np.arange(lq, dtype=np.int32)[:, None]
    j = np.arange(WINDOW + lq, dtype=np.int32)[None, :]
    dist = np.abs(i + WINDOW - j).astype(np.float64)
    return jnp.asarray((slopes.astype(np.float64)[:, None, None] * dist[None] * LOG2E).astype(np.float32))


def _mixer_call(x, state, w, *, ts, sample):
    bsz, seq, _ = x.shape
    if sample:
        nb, grid, lq, n_chunks = bsz, (1, 1), seq, 1
    else:
        nb, grid, lq, n_chunks = 1, (bsz, seq // ts), CHUNK, ts // CHUNK
    rows = nb * ts
    tile = lambda b, t: (b, t, 0)
    per_b = lambda b, t: (b, 0, 0)
    in_specs = [pl.BlockSpec((nb, ts, D_MODEL), tile)]
    args = [x]
    if sample:
        sconv, sh, kc, vc = state
        in_specs += [pl.BlockSpec((nb, RNN_CONV - 1, D_RNN), per_b), pl.BlockSpec((nb, 1, D_RNN), per_b),
                     pl.BlockSpec((nb, WINDOW, KV_W), per_b), pl.BlockSpec((nb, WINDOW, KV_W), per_b)]
        args += [sconv, sh.reshape(bsz, 1, D_RNN), kc.reshape(bsz, WINDOW, KV_W), vc.reshape(bsz, WINDOW, KV_W)]
    weights = [w['g1'], w['w_in'], w['b_gate'], w['rnn_conv_w'], w['rnn_conv_b'], w['w_gate'], w['b_a'], w['b_x'],
               w['lam'], w['q_g'], w['k_g']]
    in_specs += [_const_spec(a.shape) for a in weights]
    args += weights
    in_specs.append(pl.BlockSpec(memory_space=pltpu.SMEM))
    args.append(w['sinks'])
    tail = [_alibi_bias(lq), w['w_rnn_proj'], w['w_attn_proj'], w['w_out']]
    in_specs += [_const_spec(a.shape) for a in tail]
    args += tail

    out_shape = (jax.ShapeDtypeStruct((bsz, seq, D_MODEL), F32),
                 jax.ShapeDtypeStruct((bsz, RNN_CONV - 1, D_RNN), F32),
                 jax.ShapeDtypeStruct((bsz, 1, D_RNN), F32),
                 jax.ShapeDtypeStruct((bsz, WINDOW, KV_W), F32),
                 jax.ShapeDtypeStruct((bsz, WINDOW, KV_W), F32))
    out_specs = (pl.BlockSpec((nb, ts, D_MODEL), tile),
                 pl.BlockSpec((nb, RNN_CONV - 1, D_RNN), per_b),
                 pl.BlockSpec((nb, 1, D_RNN), per_b),
                 pl.BlockSpec((nb, WINDOW, KV_W), per_b),
                 pl.BlockSpec((nb, WINDOW, KV_W), per_b))
    scratch = [pltpu.VMEM((nb, HIST + ts, D_RNN), F32),
               pltpu.VMEM((rows, D_RNN), BF16),
               pltpu.VMEM((nb, SUBLANES, D_RNN), F32),
               pltpu.VMEM((nb, WINDOW + ts, KV_W), BF16),
               pltpu.VMEM((nb, WINDOW + ts, KV_W), BF16),
               pltpu.VMEM((rows, Q_W), BF16),
               pltpu.VMEM((rows, Q_W), BF16),
               pltpu.VMEM((rows, 2 * D_MODEL), F32),
               pltpu.VMEM((rows, D_MODEL), F32)]
    kern = functools.partial(_mixer_kernel, nb=nb, ts=ts, lq=lq, n_chunks=n_chunks, sample=sample)
    return pl.pallas_call(
        kern, out_shape=out_shape, grid=grid, in_specs=in_specs, out_specs=out_specs, scratch_shapes=scratch,
        name='mixer_sample' if sample else 'mixer_prompt',
        compiler_params=pltpu.CompilerParams(dimension_semantics=('arbitrary', 'arbitrary'),
                                             vmem_limit_bytes=VMEM_LIMIT_BYTES),
    )(*args)


def _ffn_call(x, state, w, *, ts, sample):
    bsz, seq, _ = x.shape
    if sample:
        nb, grid = bsz, (1, 1)
    else:
        nb, grid = 1, (bsz, seq // ts)
    rows = nb * ts
    tile = lambda b, t: (b, t, 0)
    per_b = lambda b, t: (b, 0, 0)
    in_specs = [pl.BlockSpec((nb, ts, D_MODEL), tile)]
    args = [x]
    if sample:
        in_specs.append(pl.BlockSpec((nb, FFN_CONV - 1, D_FF), per_b))
        args.append(state)
    weights = [w['g2'], w['w_up'], w['ffn_conv_w'], w['ffn_conv_b'], w['w_down']]
    in_specs += [_const_spec(a.shape) for a in weights]
    args += weights
    out_shape = (jax.ShapeDtypeStruct((bsz, seq, D_MODEL), F32),
                 jax.ShapeDtypeStruct((bsz, FFN_CONV - 1, D_FF), F32))
    out_specs = (pl.BlockSpec((nb, ts, D_MODEL), tile),
                 pl.BlockSpec((nb, FFN_CONV - 1, D_FF), per_b))
    scratch = [pltpu.VMEM((nb, HIST + ts, D_FF), F32),
               pltpu.VMEM((rows, D_FF), BF16)]
    kern = functools.partial(_ffn_kernel, nb=nb, ts=ts, sample=sample)
    return pl.pallas_call(
        kern, out_shape=out_shape, grid=grid, in_specs=in_specs, out_specs=out_specs, scratch_shapes=scratch,
        name='ffn_sample' if sample else 'ffn_prompt',
        compiler_params=pltpu.CompilerParams(dimension_semantics=('arbitrary', 'arbitrary'),
                                             vmem_limit_bytes=VMEM_LIMIT_BYTES),
    )(*args)


PROMPT_TILE = 512


def kernel(x_prompt, x_sample, state_rnn_conv, state_rnn_h, cache_attn_k, cache_attn_v, state_ffn_conv, norm_mix_g, w_in, b_gate, rnn_conv_w, rnn_conv_b, rnn_gate_a_w, rnn_gate_a_b, rnn_gate_x_w, rnn_gate_x_b, rnn_lambda, q_norm_g, k_norm_g, attn_sinks, w_rnn_proj, w_attn_proj, w_out, norm_ffn_g, w_up, ffn_conv_w, ffn_conv_b, w_down):
    depth = w_in.shape[0]
    xp, xs = x_prompt, x_sample
    outs = [[] for _ in range(10)]
    for l in range(depth):
        w = {
            'g1': norm_mix_g[l][None], 'w_in': w_in[l].astype(BF16), 'b_gate': b_gate[l][None],
            'rnn_conv_w': rnn_conv_w[l], 'rnn_conv_b': rnn_conv_b[l][None],
            'w_gate': jnp.concatenate([rnn_gate_a_w[l], rnn_gate_x_w[l]], axis=-1).astype(BF16),
            'b_a': rnn_gate_a_b[l][None], 'b_x': rnn_gate_x_b[l][None], 'lam': rnn_lambda[l][None],
            'q_g': q_norm_g[l][None], 'k_g': k_norm_g[l][None], 'sinks': attn_sinks[l],
            'w_rnn_proj': w_rnn_proj[l].astype(BF16), 'w_attn_proj': w_attn_proj[l].astype(BF16),
            'w_out': w_out[l].astype(BF16),
            'g2': norm_ffn_g[l][None], 'w_up': w_up[l].astype(BF16), 'ffn_conv_w': ffn_conv_w[l],
            'ffn_conv_b': ffn_conv_b[l][None], 'w_down': w_down[l].astype(BF16),
        }
        xp, conv_p, h_p, k_p, v_p = _mixer_call(xp, None, w, ts=PROMPT_TILE, sample=False)
        xp, fconv_p = _ffn_call(xp, None, w, ts=PROMPT_TILE, sample=False)
        xs, conv_s, h_s, k_s, v_s = _mixer_call(
            xs, (state_rnn_conv[l], state_rnn_h[l], cache_attn_k[l], cache_attn_v[l]), w,
            ts=xs.shape[1], sample=True)
        xs, fconv_s = _ffn_call(xs, state_ffn_conv[l], w, ts=xs.shape[1], sample=True)
        kv_shape = lambda a: a.reshape(a.shape[0], WINDOW, N_KV_HEADS, HEAD_DIM)
        for j, o in enumerate((conv_p, conv_s, h_p[:, 0], h_s[:, 0], kv_shape(k_p), kv_shape(k_s),
                               kv_shape(v_p), kv_shape(v_s), fconv_p, fconv_s)):
            outs[j].append(o)
    return (xp, xs) + tuple(jnp.stack(o) for o in outs)
```

```python
import functools
import math

import jax
import jax.numpy as jnp
import numpy as np
from jax import lax
from jax.experimental import pallas as pl
from jax.experimental.pallas import tpu as pltpu

D_MODEL = 1024
CHUNK = 64
WINDOW = 128
N_HEADS = 8
N_KV_HEADS = 2
HEAD_DIM = 128
GROUP = N_HEADS // N_KV_HEADS
ATTN_SCALE = HEAD_DIM ** -0.5
D_RNN = D_MODEL
N_RNN_BLOCKS = 8
RNN_BLOCK = D_RNN // N_RNN_BLOCKS
RNN_CONV = 4
RGLRU_C = 8.0
D_FF = ((8 * D_MODEL // 3 + 127) // 128) * 128
FFN_CONV = 3
EPS = 1e-6
Q_W = N_HEADS * HEAD_DIM
KV_W = N_KV_HEADS * HEAD_DIM
IN_W = D_RNN + Q_W + 2 * KV_W + 2 * D_MODEL
U_OFF, Q_OFF, K_OFF, V_OFF, G_OFF = 0, D_RNN, D_RNN + Q_W, D_RNN + Q_W + KV_W, D_RNN + Q_W + 2 * KV_W

SUBLANES = 8
MXU_COLS = 256
PHASES = SUBLANES
HIST = SUBLANES
VMEM_LIMIT_BYTES = 56 * 1024 * 1024
LOG2E = 1.4426950408889634
NEG_LOG2E = -LOG2E

F32 = jnp.float32
BF16 = jnp.bfloat16


def _sigmoid(x):
    return 1.0 / (1.0 + jnp.exp2(x * NEG_LOG2E))


def _sqrt_nonneg(y):
    return jnp.where(y > 0.0, y * lax.rsqrt(y), 0.0)


def _rms_rows(x, g):
    r = lax.rsqrt(jnp.mean(x * x, axis=-1, keepdims=True) + EPS)
    return x * r * g


def _dot(a, b):
    return jnp.dot(a, b, preferred_element_type=F32)


def _dot_nt(a, b):
    return lax.dot_general(a, b, (((1,), (1,)), ((), ())), preferred_element_type=F32)


def _scan_rows(a, b, h0, row_id):
    outs = []
    carry = h0
    for g in range(a.shape[0] // SUBLANES):
        ag = a[g * SUBLANES:(g + 1) * SUBLANES]
        bg = b[g * SUBLANES:(g + 1) * SUBLANES]
        for sh in (1, 2, 4):
            m = row_id >= sh
            bg = jnp.where(m, ag * pltpu.roll(bg, sh, 0) + bg, bg)
            ag = jnp.where(m, ag * pltpu.roll(ag, sh, 0), ag)
        h = ag * carry + bg
        outs.append(h)
        carry = jnp.broadcast_to(h[SUBLANES - 1:SUBLANES, :], h.shape)
    return (outs[0] if len(outs) == 1 else jnp.concatenate(outs, axis=0)), carry


def _row_groups(v):
    return [v[k * SUBLANES:(k + 1) * SUBLANES] for k in range(v.shape[0] // SUBLANES)]


def _shift_rows(v, prev_group, row_id):
    rolled = [pltpu.roll(g, 1, 0) for g in [prev_group] + _row_groups(v)]
    return jnp.concatenate([jnp.where(row_id >= 1, rolled[k + 1], rolled[k]) for k in range(len(rolled) - 1)], axis=0)


def _to_piece_order(v):
    n, c = v.shape
    return pltpu.einshape("gjd->jgd", v.reshape(n // PHASES, PHASES, c)).reshape(n, c)


def _from_piece_order(v):
    n, c = v.shape
    return pltpu.einshape("jgd->gjd", v.reshape(PHASES, n // PHASES, c)).reshape(n, c)


def _mixer_kernel(*refs, nb, ts, lq, n_chunks, sample):
    if sample:
        (x_ref, sconv_ref, sh_ref, kc_ref, vc_ref, *rest) = refs
    else:
        (x_ref, *rest) = refs
    (g1_ref, w_in_ref, bgate_ref, cw_ref, cb_ref, wg_ref, ba_ref, bx_ref, lam_ref, qg_ref, kg_ref,
     sinks_ref, bias_ref, wr_ref, wa_ref, wo_ref,
     x1_ref, conv_out_ref, h_out_ref, k_out_ref, v_out_ref,
     ubuf, ybuf, hcar, kbuf, vbuf, qbuf, yabuf, glbuf, mixbuf) = rest

    rows = nb * ts
    t = pl.program_id(1)

    if not sample:
        @pl.when(t == 0)
        def _():
            ubuf[...] = jnp.zeros_like(ubuf)
            hcar[...] = jnp.zeros_like(hcar)
            kbuf[:, 0:WINDOW, :] = jnp.zeros((nb, WINDOW, KV_W), BF16)
            vbuf[:, 0:WINDOW, :] = jnp.zeros((nb, WINDOW, KV_W), BF16)

    x = x_ref[...].reshape(rows, D_MODEL)
    hb = _rms_rows(x, g1_ref[...]).astype(BF16)

    lam = lam_ref[...]
    neg_c_softplus = -RGLRU_C * (jnp.maximum(-lam, 0.0) + jnp.log1p(jnp.exp(-jnp.abs(lam))))
    row_id = lax.broadcasted_iota(jnp.int32, (SUBLANES, MXU_COLS), 0)

    def project_q(half):
        q = _dot(hb, w_in_ref[:, Q_OFF + half * 512:Q_OFF + (half + 1) * 512])
        for j in range(4):
            hd = half * 4 + j
            qbuf[:, hd * HEAD_DIM:(hd + 1) * HEAD_DIM] = _rms_rows(
                q[:, j * HEAD_DIM:(j + 1) * HEAD_DIM], qg_ref[...]).astype(BF16)

    def project_kv():
        kvp = _dot(hb, w_in_ref[:, K_OFF:K_OFF + 2 * KV_W])
        kn = jnp.concatenate(
            [_rms_rows(kvp[:, j * HEAD_DIM:(j + 1) * HEAD_DIM], kg_ref[...]) for j in range(N_KV_HEADS)], axis=1)
        v = kvp[:, KV_W:2 * KV_W]
        for s in range(nb):
            if sample:
                kbuf[s, 0:WINDOW, :] = kc_ref[s].astype(BF16)
                vbuf[s, 0:WINDOW, :] = vc_ref[s].astype(BF16)
                k_out_ref[s, 0:WINDOW - ts, :] = kc_ref[s, ts:WINDOW, :]
                v_out_ref[s, 0:WINDOW - ts, :] = vc_ref[s, ts:WINDOW, :]
                k_out_ref[s, WINDOW - ts:WINDOW, :] = kn[s * ts:(s + 1) * ts]
                v_out_ref[s, WINDOW - ts:WINDOW, :] = v[s * ts:(s + 1) * ts]
            else:
                k_out_ref[s] = kn[(s + 1) * ts - WINDOW:(s + 1) * ts]
                v_out_ref[s] = v[(s + 1) * ts - WINDOW:(s + 1) * ts]
            kbuf[s, WINDOW:WINDOW + ts, :] = kn[s * ts:(s + 1) * ts].astype(BF16)
            vbuf[s, WINDOW:WINDOW + ts, :] = v[s * ts:(s + 1) * ts].astype(BF16)

    def project_gate(part):
        glbuf[:, part * 512:(part + 1) * 512] = _dot(
            hb, w_in_ref[:, G_OFF + part * 512:G_OFF + (part + 1) * 512])

    side_work = [[lambda: project_q(1)],
                 [project_kv, lambda: project_gate(0)],
                 [lambda: project_gate(1)],
                 [lambda: project_gate(2)]]
    tail_work = [lambda: project_gate(3)]

    n_blocks = D_RNN // MXU_COLS
    per_block = MXU_COLS // RNN_BLOCK

    def u_proj(m):
        return _dot(hb, w_in_ref[:, U_OFF + m * MXU_COLS:U_OFF + (m + 1) * MXU_COLS])

    g_rows = ts // PHASES
    hist = RNN_CONV - 1

    def conv_block_pieces(m, u):
        cs = slice(m * MXU_COLS, (m + 1) * MXU_COLS)
        u = _to_piece_order(u)
        shifted = []
        for k in range(hist):
            pc = PHASES - hist + k
            piece = u[pc * g_rows:(pc + 1) * g_rows]
            shifted.append(_shift_rows(piece, ubuf[k, :, cs], row_id))
            ubuf[k, :, cs] = piece[g_rows - SUBLANES:g_rows]
            conv_out_ref[0, k:k + 1, cs] = piece[g_rows - 1:g_rows]
        ext = jnp.concatenate(shifted + [u], axis=0)
        acc = ext[hist * g_rows:hist * g_rows + ts] * cw_ref[hist:hist + 1, cs]
        for k in range(hist):
            acc = acc + ext[k * g_rows:k * g_rows + ts] * cw_ref[k:k + 1, cs]
        return acc + cb_ref[:, cs]

    def scan_block_pieces(m, a_all, b_all):
        cs = slice(m * MXU_COLS, (m + 1) * MXU_COLS)
        h0 = hcar[0, :, cs]
        loc = b_all[0:g_rows]
        dec = a_all[0:g_rows]
        locs, decs = [loc], [dec]
        for j in range(1, PHASES):
            aj = a_all[j * g_rows:(j + 1) * g_rows]
            loc = aj * loc + b_all[j * g_rows:(j + 1) * g_rows]
            dec = aj * dec
            locs.append(loc)
            decs.append(dec)
        ends, carry = _scan_rows(dec, loc, h0, row_id)
        entry = _shift_rows(ends, h0, row_id)
        outs = [locs[j] + decs[j] * entry for j in range(PHASES - 1)] + [ends]
        ybuf[:, cs] = _from_piece_order(jnp.concatenate(outs, axis=0)).astype(BF16)
        h_out_ref[0, :, cs] = carry[0:1, :]
        hcar[0, :, cs] = carry

    def conv_block(m, u):
        if not sample:
            return conv_block_pieces(m, u)
        cs = slice(m * MXU_COLS, (m + 1) * MXU_COLS)
        ucs = []
        for s in range(nb):
            ubuf[s, HIST - hist:HIST, cs] = sconv_ref[s, :, cs]
            ubuf[s, HIST:HIST + ts, cs] = u[s * ts:(s + 1) * ts]
            acc = u[s * ts:(s + 1) * ts] * cw_ref[hist:hist + 1, cs]
            for j in range(hist):
                acc = acc + ubuf[s, HIST - hist + j:HIST - hist + j + ts, cs] * cw_ref[j:j + 1, cs]
            ucs.append(acc + cb_ref[:, cs])
            conv_out_ref[s, :, cs] = ubuf[s, HIST + ts - hist:HIST + ts, cs]
        return jnp.concatenate(ucs, axis=0)

    def gate_dots(m, ucb):
        return [_dot(ucb[:, j * RNN_BLOCK:(j + 1) * RNN_BLOCK], wg_ref[m * per_block + j]) for j in range(per_block)]

    def gate_math(m, zs, uc):
        a_parts, b_parts = [], []
        for j, z in enumerate(zs):
            n = m * per_block + j
            sl = slice(n * RNN_BLOCK, (n + 1) * RNN_BLOCK)
            r = _sigmoid(z[:, :RNN_BLOCK] + ba_ref[:, sl])
            i = _sigmoid(z[:, RNN_BLOCK:] + bx_ref[:, sl])
            a = jnp.exp(neg_c_softplus[:, sl] * r)
            a_parts.append(a)
            b_parts.append(_sqrt_nonneg(1.0 - a * a) * (i * uc[:, j * RNN_BLOCK:(j + 1) * RNN_BLOCK]))
        return jnp.concatenate(a_parts, axis=1), jnp.concatenate(b_parts, axis=1)

    def scan_block(m, a_all, b_all):
        if not sample:
            return scan_block_pieces(m, a_all, b_all)
        cs = slice(m * MXU_COLS, (m + 1) * MXU_COLS)
        for s in range(nb):
            h0 = jnp.broadcast_to(sh_ref[s, :, cs], (SUBLANES, MXU_COLS))
            y, hl = _scan_rows(a_all[s * ts:(s + 1) * ts], b_all[s * ts:(s + 1) * ts], h0, row_id)
            ybuf[s * ts:(s + 1) * ts, cs] = y.astype(BF16)
            h_out_ref[s, :, cs] = hl[0:1, :]

    u_first = u_proj(0)
    project_q(0)
    uc = conv_block(0, u_first)
    for m in range(n_blocks):
        zs = gate_dots(m, uc.astype(BF16))
        u_next = u_proj(m + 1) if m + 1 < n_blocks else None
        for work in side_work[m]:
            work()
        a_all, b_all = gate_math(m, zs, uc)
        if u_next is not None:
            uc = conv_block(m + 1, u_next)
        else:
            for work in tail_work:
                work()
        scan_block(m, a_all, b_all)

    lk = WINDOW + lq
    col_id = lax.broadcasted_iota(jnp.int32, (lq, lk), 1)
    units = [(s, c, kv) for s in range(nb) for c in range(n_chunks) for kv in range(N_KV_HEADS)]
    n_pieces = D_MODEL // MXU_COLS
    wr_at = {(len(units) * p) // n_pieces: p for p in range(n_pieces)}

    def rnn_proj(p):
        ps = slice(p * MXU_COLS, (p + 1) * MXU_COLS)
        g_rnn = _sigmoid(glbuf[:, ps] + bgate_ref[:, ps])
        mixbuf[:, ps] = g_rnn * _dot(ybuf[...], wr_ref[:, ps])

    def scores(unit):
        s, c, kv = unit
        r0 = s * ts + c * lq
        kk = kbuf[s, c * lq:c * lq + lk, kv * HEAD_DIM:(kv + 1) * HEAD_DIM]
        qs = jnp.concatenate(
            [qbuf[r0:r0 + lq, (kv * GROUP + g) * HEAD_DIM:(kv * GROUP + g + 1) * HEAD_DIM]
             for g in range(GROUP)], axis=0)
        return _dot_nt(qs, kk)

    def softmax_pv(unit, sc):
        s, c, kv = unit
        r0 = s * ts + c * lq
        vv = vbuf[s, c * lq:c * lq + lk, kv * HEAD_DIM:(kv + 1) * HEAD_DIM]
        sc = sc * (ATTN_SCALE * LOG2E)
        ps = []
        dens = []
        for g in range(GROUP):
            hd = kv * GROUP + g
            sg = sc[g * lq:(g + 1) * lq] - bias_ref[hd]
            if (not sample) and c * lq < WINDOW:
                first_valid = jnp.where(t == 0, WINDOW - c * lq, 0)
                sg = jnp.where(col_id >= first_valid, sg, -jnp.inf)
            sink = sinks_ref[hd] * LOG2E
            mx = jnp.maximum(jnp.max(sg, axis=-1, keepdims=True), sink)
            p = jnp.exp2(sg - mx)
            dens.append(jnp.sum(p, axis=-1, keepdims=True) + jnp.exp2(sink - mx))
            ps.append(p.astype(BF16))
        o = _dot(jnp.concatenate(ps, axis=0), vv)
        for g in range(GROUP):
            hd = kv * GROUP + g
            yabuf[r0:r0 + lq, hd * HEAD_DIM:(hd + 1) * HEAD_DIM] = (o[g * lq:(g + 1) * lq] / dens[g]).astype(BF16)

    sc = scores(units[0])
    for idx, unit in enumerate(units):
        sc_next = scores(units[idx + 1]) if idx + 1 < len(units) else None
        if idx in wr_at:
            rnn_proj(wr_at[idx])
        softmax_pv(unit, sc)
        sc = sc_next
    if not sample:
        for s in range(nb):
            kbuf[s, 0:WINDOW, :] = kbuf[s, ts:ts + WINDOW, :]
            vbuf[s, 0:WINDOW, :] = vbuf[s, ts:ts + WINDOW, :]

    attn_proj = [_dot(yabuf[...], wa_ref[:, p * MXU_COLS:(p + 1) * MXU_COLS]) for p in range(n_pieces)]
    acc = None
    for p in range(n_pieces):
        ps = slice(p * MXU_COLS, (p + 1) * MXU_COLS)
        gs = slice(D_MODEL + p * MXU_COLS, D_MODEL + (p + 1) * MXU_COLS)
        g_attn = _sigmoid(glbuf[:, gs] + bgate_ref[:, gs])
        mixed = mixbuf[:, ps] + g_attn * attn_proj[p]
        d = _dot(mixed.astype(BF16), wo_ref[ps, :])
        acc = d if acc is None else acc + d
    x1_ref[...] = (x + acc).reshape(nb, ts, D_MODEL)


def _ffn_kernel(*refs, nb, ts, sample):
    if sample:
        (x_ref, sconv_ref, *rest) = refs
    else:
        (x_ref, *rest) = refs
    (g2_ref, wup_ref, cw_ref, cb_ref, wdn_ref, y_ref, conv_out_ref, abuf, hbuf) = rest
    rows = nb * ts
    t = pl.program_id(1)

    if not sample:
        @pl.when(t == 0)
        def _():
            abuf[:, 0:HIST, :] = jnp.zeros((nb, HIST, D_FF), F32)

    x = x_ref[...].reshape(rows, D_MODEL)
    hb = _rms_rows(x, g2_ref[...]).astype(BF16)
    a = _dot(hb, wup_ref[:, 0:D_FF])
    for s in range(nb):
        if sample:
            abuf[s, HIST - (FFN_CONV - 1):HIST, :] = sconv_ref[s]
        abuf[s, HIST:HIST + ts, :] = a[s * ts:(s + 1) * ts]
    acs = []
    for s in range(nb):
        acc = abuf[s, HIST:HIST + ts, :] * cw_ref[FFN_CONV - 1:FFN_CONV, :]
        for j in range(FFN_CONV - 1):
            off = HIST - (FFN_CONV - 1) + j
            acc = acc + abuf[s, off:off + ts, :] * cw_ref[j:j + 1, :]
        acs.append(acc + cb_ref[...])
        conv_out_ref[s] = abuf[s, HIST + ts - (FFN_CONV - 1):HIST + ts, :]
        if not sample:
            abuf[s, 0:HIST, :] = abuf[s, ts:ts + HIST, :]
    ac = acs[0] if nb == 1 else jnp.concatenate(acs, axis=0)
    b = _dot(hb, wup_ref[:, D_FF:2 * D_FF])
    gelu = 0.5 * ac * (1.0 + lax.erf(ac * np.float32(math.sqrt(0.5))))
    hbuf[...] = (gelu * b).astype(BF16)
    out = x + _dot(hbuf[...], wdn_ref[...])
    y_ref[...] = out.reshape(nb, ts, D_MODEL)


def _const_spec(shape):
    nd = len(shape)
    return pl.BlockSpec(shape, lambda b, t: (0,) * nd, pipeline_mode=pl.Buffered(1))


def _alibi_bias(lq):
    slopes = np.array([2.0 ** (-8.0 * (h + 1) / N_HEADS) for h in range(N_HEADS)], dtype=np.float32)
    i = np.arange(lq, dtype=np.int32)[:, None]
    j = np.arange(WINDOW + lq, dtype=np.int32)[None, :]
    dist = np.abs(i + WINDOW - j).astype(np.float64)
    return jnp.asarray((slopes.astype(np.float64)[:, None, None] * dist[None] * LOG2E).astype(np.float32))


def _mixer_call(x, state, w, *, ts, sample):
    bsz, seq, _ = x.shape
    if sample:
        nb, grid, lq, n_chunks = bsz, (1, 1), seq, 1
    else:
        nb, grid, lq, n_chunks = 1, (bsz, seq // ts), CHUNK, ts // CHUNK
    rows = nb * ts
    tile = lambda b, t: (b, t, 0)
    per_b = lambda b, t: (b, 0, 0)
    in_specs = [pl.BlockSpec((nb, ts, D_MODEL), tile)]
    args = [x]
    if sample:
        sconv, sh, kc, vc = state
        in_specs += [pl.BlockSpec((nb, RNN_CONV - 1, D_RNN), per_b), pl.BlockSpec((nb, 1, D_RNN), per_b),
                     pl.BlockSpec((nb, WINDOW, KV_W), per_b), pl.BlockSpec((nb, WINDOW, KV_W), per_b)]
        args += [sconv, sh.reshape(bsz, 1, D_RNN), kc.reshape(bsz, WINDOW, KV_W), vc.reshape(bsz, WINDOW, KV_W)]
    weights = [w['g1'], w['w_in'], w['b_gate'], w['rnn_conv_w'], w['rnn_conv_b'], w['w_gate'], w['b_a'], w['b_x'],
               w['lam'], w['q_g'], w['k_g']]
    in_specs += [_const_spec(a.shape) for a in weights]
    args += weights
    in_specs.append(pl.BlockSpec(memory_space=pltpu.SMEM))
    args.append(w['sinks'])
    tail = [_alibi_bias(lq), w['w_rnn_proj'], w['w_attn_proj'], w['w_out']]
    in_specs += [_const_spec(a.shape) for a in tail]
    args += tail

    out_shape = (jax.ShapeDtypeStruct((bsz, seq, D_MODEL), F32),
                 jax.ShapeDtypeStruct((bsz, RNN_CONV - 1, D_RNN), F32),
                 jax.ShapeDtypeStruct((bsz, 1, D_RNN), F32),
                 jax.ShapeDtypeStruct((bsz, WINDOW, KV_W), F32),
                 jax.ShapeDtypeStruct((bsz, WINDOW, KV_W), F32))
    out_specs = (pl.BlockSpec((nb, ts, D_MODEL), tile),
                 pl.BlockSpec((nb, RNN_CONV - 1, D_RNN), per_b),
                 pl.BlockSpec((nb, 1, D_RNN), per_b),
                 pl.BlockSpec((nb, WINDOW, KV_W), per_b),
                 pl.BlockSpec((nb, WINDOW, KV_W), per_b))
    conv_hist = (nb, HIST + ts, D_RNN) if sample else (RNN_CONV - 1, SUBLANES, D_RNN)
    scratch = [pltpu.VMEM(conv_hist, F32),
               pltpu.VMEM((rows, D_RNN), BF16),
               pltpu.VMEM((nb, SUBLANES, D_RNN), F32),
               pltpu.VMEM((nb, WINDOW + ts, KV_W), BF16),
               pltpu.VMEM((nb, WINDOW + ts, KV_W), BF16),
               pltpu.VMEM((rows, Q_W), BF16),
               pltpu.VMEM((rows, Q_W), BF16),
               pltpu.VMEM((rows, 2 * D_MODEL), F32),
               pltpu.VMEM((rows, D_MODEL), F32)]
    kern = functools.partial(_mixer_kernel, nb=nb, ts=ts, lq=lq, n_chunks=n_chunks, sample=sample)
    return pl.pallas_call(
        kern, out_shape=out_shape, grid=grid, in_specs=in_specs, out_specs=out_specs, scratch_shapes=scratch,
        name='mixer_sample' if sample else 'mixer_prompt',
        compiler_params=pltpu.CompilerParams(dimension_semantics=('arbitrary', 'arbitrary'),
                                             vmem_limit_bytes=VMEM_LIMIT_BYTES),
    )(*args)


def _ffn_call(x, state, w, *, ts, sample):
    bsz, seq, _ = x.shape
    if sample:
        nb, grid = bsz, (1, 1)
    else:
        nb, grid = 1, (bsz, seq // ts)
    rows = nb * ts
    tile = lambda b, t: (b, t, 0)
    per_b = lambda b, t: (b, 0, 0)
    in_specs = [pl.BlockSpec((nb, ts, D_MODEL), tile)]
    args = [x]
    if sample:
        in_specs.append(pl.BlockSpec((nb, FFN_CONV - 1, D_FF), per_b))
        args.append(state)
    weights = [w['g2'], w['w_up'], w['ffn_conv_w'], w['ffn_conv_b'], w['w_down']]
    in_specs += [_const_spec(a.shape) for a in weights]
    args += weights
    out_shape = (jax.ShapeDtypeStruct((bsz, seq, D_MODEL), F32),
                 jax.ShapeDtypeStruct((bsz, FFN_CONV - 1, D_FF), F32))
    out_specs = (pl.BlockSpec((nb, ts, D_MODEL), tile),
                 pl.BlockSpec((nb, FFN_CONV - 1, D_FF), per_b))
    scratch = [pltpu.VMEM((nb, HIST + ts, D_FF), F32),
               pltpu.VMEM((rows, D_FF), BF16)]
    kern = functools.partial(_ffn_kernel, nb=nb, ts=ts, sample=sample)
    return pl.pallas_call(
        kern, out_shape=out_shape, grid=grid, in_specs=in_specs, out_specs=out_specs, scratch_shapes=scratch,
        name='ffn_sample' if sample else 'ffn_prompt',
        compiler_params=pltpu.CompilerParams(dimension_semantics=('arbitrary', 'arbitrary'),
                                             vmem_limit_bytes=VMEM_LIMIT_BYTES),
    )(*args)


PROMPT_TILE = 512


def kernel(x_prompt, x_sample, state_rnn_conv, state_rnn_h, cache_attn_k, cache_attn_v, state_ffn_conv, norm_mix_g, w_in, b_gate, rnn_conv_w, rnn_conv_b, rnn_gate_a_w, rnn_gate_a_b, rnn_gate_x_w, rnn_gate_x_b, rnn_lambda, q_norm_g, k_norm_g, attn_sinks, w_rnn_proj, w_attn_proj, w_out, norm_ffn_g, w_up, ffn_conv_w, ffn_conv_b, w_down):
    depth = w_in.shape[0]
    xp, xs = x_prompt, x_sample
    outs = [[] for _ in range(10)]
    for l in range(depth):
        w = {
            'g1': norm_mix_g[l][None], 'w_in': w_in[l].astype(BF16), 'b_gate': b_gate[l][None],
            'rnn_conv_w': rnn_conv_w[l], 'rnn_conv_b': rnn_conv_b[l][None],
            'w_gate': jnp.concatenate([rnn_gate_a_w[l], rnn_gate_x_w[l]], axis=-1).astype(BF16),
            'b_a': rnn_gate_a_b[l][None], 'b_x': rnn_gate_x_b[l][None], 'lam': rnn_lambda[l][None],
            'q_g': q_norm_g[l][None], 'k_g': k_norm_g[l][None], 'sinks': attn_sinks[l],
            'w_rnn_proj': w_rnn_proj[l].astype(BF16), 'w_attn_proj': w_attn_proj[l].astype(BF16),
            'w_out': w_out[l].astype(BF16),
            'g2': norm_ffn_g[l][None], 'w_up': w_up[l].astype(BF16), 'ffn_conv_w': ffn_conv_w[l],
            'ffn_conv_b': ffn_conv_b[l][None], 'w_down': w_down[l].astype(BF16),
        }
        xp, conv_p, h_p, k_p, v_p = _mixer_call(xp, None, w, ts=PROMPT_TILE, sample=False)
        xp, fconv_p = _ffn_call(xp, None, w, ts=PROMPT_TILE, sample=False)
        xs, conv_s, h_s, k_s, v_s = _mixer_call(
            xs, (state_rnn_conv[l], state_rnn_h[l], cache_attn_k[l], cache_attn_v[l]), w,
            ts=xs.shape[1], sample=True)
        xs, fconv_s = _ffn_call(xs, state_ffn_conv[l], w, ts=xs.shape[1], sample=True)
        kv_shape = lambda a: a.reshape(a.shape[0], WINDOW, N_KV_HEADS, HEAD_DIM)
        for j, o in enumerate((conv_p, conv_s, h_p[:, 0], h_s[:, 0], kv_shape(k_p), kv_shape(k_s),
                               kv_shape(v_p), kv_shape(v_s), fconv_p, fconv_s)):
            outs[j].append(o)
    return (xp, xs) + tuple(jnp.stack(o) for o in outs)
```

```python
import functools
import math

import jax
import jax.numpy as jnp
import numpy as np
from jax import lax
from jax.experimental import pallas as pl
from jax.experimental.pallas import tpu as pltpu

D_MODEL = 1024
CHUNK = 64
WINDOW = 128
N_HEADS = 8
N_KV_HEADS = 2
HEAD_DIM = 128
GROUP = N_HEADS // N_KV_HEADS
ATTN_SCALE = HEAD_DIM ** -0.5
D_RNN = D_MODEL
N_RNN_BLOCKS = 8
RNN_BLOCK = D_RNN // N_RNN_BLOCKS
RNN_CONV = 4
RGLRU_C = 8.0
D_FF = ((8 * D_MODEL // 3 + 127) // 128) * 128
FFN_CONV = 3
EPS = 1e-6
Q_W = N_HEADS * HEAD_DIM
KV_W = N_KV_HEADS * HEAD_DIM
IN_W = D_RNN + Q_W + 2 * KV_W + 2 * D_MODEL
U_OFF, Q_OFF, K_OFF, V_OFF, G_OFF = 0, D_RNN, D_RNN + Q_W, D_RNN + Q_W + KV_W, D_RNN + Q_W + 2 * KV_W

SUBLANES = 8
MXU_COLS = 256
PHASES = SUBLANES
HIST = SUBLANES
VMEM_LIMIT_BYTES = 56 * 1024 * 1024
LOG2E = 1.4426950408889634
NEG_LOG2E = -LOG2E

F32 = jnp.float32
BF16 = jnp.bfloat16


def _sigmoid(x):
    return 1.0 / (1.0 + jnp.exp2(x * NEG_LOG2E))


def _sqrt_nonneg(y):
    return jnp.where(y > 0.0, y * lax.rsqrt(y), 0.0)


def _rms_rows(x, g):
    r = lax.rsqrt(jnp.mean(x * x, axis=-1, keepdims=True) + EPS)
    return x * r * g


def _dot(a, b):
    return jnp.dot(a, b, preferred_element_type=F32)


def _dot_nt(a, b):
    return lax.dot_general(a, b, (((1,), (1,)), ((), ())), preferred_element_type=F32)


def _scan_rows(a, b, h0, row_id):
    outs = []
    carry = h0
    for g in range(a.shape[0] // SUBLANES):
        ag = a[g * SUBLANES:(g + 1) * SUBLANES]
        bg = b[g * SUBLANES:(g + 1) * SUBLANES]
        for sh in (1, 2, 4):
            m = row_id >= sh
            bg = jnp.where(m, ag * pltpu.roll(bg, sh, 0) + bg, bg)
            ag = jnp.where(m, ag * pltpu.roll(ag, sh, 0), ag)
        h = ag * carry + bg
        outs.append(h)
        carry = jnp.broadcast_to(h[SUBLANES - 1:SUBLANES, :], h.shape)
    return (outs[0] if len(outs) == 1 else jnp.concatenate(outs, axis=0)), carry


def _row_groups(v):
    return [v[k * SUBLANES:(k + 1) * SUBLANES] for k in range(v.shape[0] // SUBLANES)]


def _shift_rows(v, prev_group, row_id):
    rolled = [pltpu.roll(g, 1, 0) for g in [prev_group] + _row_groups(v)]
    return jnp.concatenate([jnp.where(row_id >= 1, rolled[k + 1], rolled[k]) for k in range(len(rolled) - 1)], axis=0)


def _to_piece_order(v):
    n, c = v.shape
    return jnp.swapaxes(v.reshape(n // PHASES, PHASES, c), 0, 1).reshape(n, c)


def _from_piece_order(v):
    n, c = v.shape
    return jnp.swapaxes(v.reshape(PHASES, n // PHASES, c), 0, 1).reshape(n, c)


def _mixer_kernel(*refs, nb, ts, lq, n_chunks, sample):
    if sample:
        (x_ref, sconv_ref, sh_ref, kc_ref, vc_ref, *rest) = refs
    else:
        (x_ref, *rest) = refs
    (g1_ref, w_in_ref, bgate_ref, cw_ref, cb_ref, wg_ref, ba_ref, bx_ref, lam_ref, qg_ref, kg_ref,
     sinks_ref, bias_ref, wr_ref, wa_ref, wo_ref,
     x1_ref, conv_out_ref, h_out_ref, k_out_ref, v_out_ref,
     ubuf, ybuf, hcar, kbuf, vbuf, qbuf, yabuf, glbuf, mixbuf) = rest

    rows = nb * ts
    t = pl.program_id(1)

    if not sample:
        @pl.when(t == 0)
        def _():
            ubuf[...] = jnp.zeros_like(ubuf)
            hcar[...] = jnp.zeros_like(hcar)
            kbuf[:, 0:WINDOW, :] = jnp.zeros((nb, WINDOW, KV_W), BF16)
            vbuf[:, 0:WINDOW, :] = jnp.zeros((nb, WINDOW, KV_W), BF16)

    x = x_ref[...].reshape(rows, D_MODEL)
    hb = _rms_rows(x, g1_ref[...]).astype(BF16)

    lam = lam_ref[...]
    neg_c_softplus = -RGLRU_C * (jnp.maximum(-lam, 0.0) + jnp.log1p(jnp.exp(-jnp.abs(lam))))
    row_id = lax.broadcasted_iota(jnp.int32, (SUBLANES, MXU_COLS), 0)

    def project_q(half):
        q = _dot(hb, w_in_ref[:, Q_OFF + half * 512:Q_OFF + (half + 1) * 512])
        for j in range(4):
            hd = half * 4 + j
            qbuf[:, hd * HEAD_DIM:(hd + 1) * HEAD_DIM] = _rms_rows(
                q[:, j * HEAD_DIM:(j + 1) * HEAD_DIM], qg_ref[...]).astype(BF16)

    def project_kv():
        kvp = _dot(hb, w_in_ref[:, K_OFF:K_OFF + 2 * KV_W])
        kn = jnp.concatenate(
            [_rms_rows(kvp[:, j * HEAD_DIM:(j + 1) * HEAD_DIM], kg_ref[...]) for j in range(N_KV_HEADS)], axis=1)
        v = kvp[:, KV_W:2 * KV_W]
        for s in range(nb):
            if sample:
                kbuf[s, 0:WINDOW, :] = kc_ref[s].astype(BF16)
                vbuf[s, 0:WINDOW, :] = vc_ref[s].astype(BF16)
                k_out_ref[s, 0:WINDOW - ts, :] = kc_ref[s, ts:WINDOW, :]
                v_out_ref[s, 0:WINDOW - ts, :] = vc_ref[s, ts:WINDOW, :]
                k_out_ref[s, WINDOW - ts:WINDOW, :] = kn[s * ts:(s + 1) * ts]
                v_out_ref[s, WINDOW - ts:WINDOW, :] = v[s * ts:(s + 1) * ts]
            else:
                k_out_ref[s] = kn[(s + 1) * ts - WINDOW:(s + 1) * ts]
                v_out_ref[s] = v[(s + 1) * ts - WINDOW:(s + 1) * ts]
            kbuf[s, WINDOW:WINDOW + ts, :] = kn[s * ts:(s + 1) * ts].astype(BF16)
            vbuf[s, WINDOW:WINDOW + ts, :] = v[s * ts:(s + 1) * ts].astype(BF16)

    def project_gate(part):
        glbuf[:, part * 512:(part + 1) * 512] = _dot(
            hb, w_in_ref[:, G_OFF + part * 512:G_OFF + (part + 1) * 512])

    side_work = [[lambda: project_q(1)],
                 [project_kv, lambda: project_gate(0)],
                 [lambda: project_gate(1)],
                 [lambda: project_gate(2)]]
    tail_work = [lambda: project_gate(3)]

    n_blocks = D_RNN // MXU_COLS
    per_block = MXU_COLS // RNN_BLOCK

    def u_proj(m):
        return _dot(hb, w_in_ref[:, U_OFF + m * MXU_COLS:U_OFF + (m + 1) * MXU_COLS])

    g_rows = ts // PHASES
    hist = RNN_CONV - 1

    def conv_block_pieces(m, u):
        cs = slice(m * MXU_COLS, (m + 1) * MXU_COLS)
        u = _to_piece_order(u)
        shifted = []
        for k in range(hist):
            pc = PHASES - hist + k
            piece = u[pc * g_rows:(pc + 1) * g_rows]
            shifted.append(_shift_rows(piece, ubuf[k, :, cs], row_id))
            ubuf[k, :, cs] = piece[g_rows - SUBLANES:g_rows]
            conv_out_ref[0, k:k + 1, cs] = piece[g_rows - 1:g_rows]
        ext = jnp.concatenate(shifted + [u], axis=0)
        acc = ext[hist * g_rows:hist * g_rows + ts] * cw_ref[hist:hist + 1, cs]
        for k in range(hist):
            acc = acc + ext[k * g_rows:k * g_rows + ts] * cw_ref[k:k + 1, cs]
        return acc + cb_ref[:, cs]

    def scan_block_pieces(m, a_all, b_all):
        cs = slice(m * MXU_COLS, (m + 1) * MXU_COLS)
        h0 = hcar[0, :, cs]
        loc = b_all[0:g_rows]
        dec = a_all[0:g_rows]
        locs, decs = [loc], [dec]
        for j in range(1, PHASES):
            aj = a_all[j * g_rows:(j + 1) * g_rows]
            loc = aj * loc + b_all[j * g_rows:(j + 1) * g_rows]
            dec = aj * dec
            locs.append(loc)
            decs.append(dec)
        ends, carry = _scan_rows(dec, loc, h0, row_id)
        entry = _shift_rows(ends, h0, row_id)
        outs = [locs[j] + decs[j] * entry for j in range(PHASES - 1)] + [ends]
        ybuf[:, cs] = _from_piece_order(jnp.concatenate(outs, axis=0)).astype(BF16)
        h_out_ref[0, :, cs] = carry[0:1, :]
        hcar[0, :, cs] = carry

    def conv_block(m, u):
        if not sample:
            return conv_block_pieces(m, u)
        cs = slice(m * MXU_COLS, (m + 1) * MXU_COLS)
        ucs = []
        for s in range(nb):
            ubuf[s, HIST - hist:HIST, cs] = sconv_ref[s, :, cs]
            ubuf[s, HIST:HIST + ts, cs] = u[s * ts:(s + 1) * ts]
            acc = u[s * ts:(s + 1) * ts] * cw_ref[hist:hist + 1, cs]
            for j in range(hist):
                acc = acc + ubuf[s, HIST - hist + j:HIST - hist + j + ts, cs] * cw_ref[j:j + 1, cs]
            ucs.append(acc + cb_ref[:, cs])
            conv_out_ref[s, :, cs] = ubuf[s, HIST + ts - hist:HIST + ts, cs]
        return jnp.concatenate(ucs, axis=0)

    def gate_dots(m, ucb):
        return [_dot(ucb[:, j * RNN_BLOCK:(j + 1) * RNN_BLOCK], wg_ref[m * per_block + j]) for j in range(per_block)]

    def gate_math(m, zs, uc):
        a_parts, b_parts = [], []
        for j, z in enumerate(zs):
            n = m * per_block + j
            sl = slice(n * RNN_BLOCK, (n + 1) * RNN_BLOCK)
            r = _sigmoid(z[:, :RNN_BLOCK] + ba_ref[:, sl])
            i = _sigmoid(z[:, RNN_BLOCK:] + bx_ref[:, sl])
            a = jnp.exp(neg_c_softplus[:, sl] * r)
            a_parts.append(a)
            b_parts.append(_sqrt_nonneg(1.0 - a * a) * (i * uc[:, j * RNN_BLOCK:(j + 1) * RNN_BLOCK]))
        return jnp.concatenate(a_parts, axis=1), jnp.concatenate(b_parts, axis=1)

    def scan_block(m, a_all, b_all):
        if not sample:
            return scan_block_pieces(m, a_all, b_all)
        cs = slice(m * MXU_COLS, (m + 1) * MXU_COLS)
        for s in range(nb):
            h0 = jnp.broadcast_to(sh_ref[s, :, cs], (SUBLANES, MXU_COLS))
            y, hl = _scan_rows(a_all[s * ts:(s + 1) * ts], b_all[s * ts:(s + 1) * ts], h0, row_id)
            ybuf[s * ts:(s + 1) * ts, cs] = y.astype(BF16)
            h_out_ref[s, :, cs] = hl[0:1, :]

    u_first = u_proj(0)
    project_q(0)
    uc = conv_block(0, u_first)
    for m in range(n_blocks):
        zs = gate_dots(m, uc.astype(BF16))
        u_next = u_proj(m + 1) if m + 1 < n_blocks else None
        for work in side_work[m]:
            work()
        a_all, b_all = gate_math(m, zs, uc)
        if u_next is not None:
            uc = conv_block(m + 1, u_next)
        else:
            for work in tail_work:
                work()
        scan_block(m, a_all, b_all)

    lk = WINDOW + lq
    col_id = lax.broadcasted_iota(jnp.int32, (lq, lk), 1)
    units = [(s, c, kv) for s in range(nb) for c in range(n_chunks) for kv in range(N_KV_HEADS)]
    n_pieces = D_MODEL // MXU_COLS
    wr_at = {(len(units) * p) // n_pieces: p for p in range(n_pieces)}

    def rnn_proj(p):
        ps = slice(p * MXU_COLS, (p + 1) * MXU_COLS)
        g_rnn = _sigmoid(glbuf[:, ps] + bgate_ref[:, ps])
        mixbuf[:, ps] = g_rnn * _dot(ybuf[...], wr_ref[:, ps])

    def scores(unit):
        s, c, kv = unit
        r0 = s * ts + c * lq
        kk = kbuf[s, c * lq:c * lq + lk, kv * HEAD_DIM:(kv + 1) * HEAD_DIM]
        qs = jnp.concatenate(
            [qbuf[r0:r0 + lq, (kv * GROUP + g) * HEAD_DIM:(kv * GROUP + g + 1) * HEAD_DIM]
             for g in range(GROUP)], axis=0)
        return _dot_nt(qs, kk)

    def softmax_pv(unit, sc):
        s, c, kv = unit
        r0 = s * ts + c * lq
        vv = vbuf[s, c * lq:c * lq + lk, kv * HEAD_DIM:(kv + 1) * HEAD_DIM]
        sc = sc * (ATTN_SCALE * LOG2E)
        ps = []
        dens = []
        for g in range(GROUP):
            hd = kv * GROUP + g
            sg = sc[g * lq:(g + 1) * lq] - bias_ref[hd]
            if (not sample) and c * lq < WINDOW:
                first_valid = jnp.where(t == 0, WINDOW - c * lq, 0)
                sg = jnp.where(col_id >= first_valid, sg, -jnp.inf)
            sink = sinks_ref[hd] * LOG2E
            mx = jnp.maximum(jnp.max(sg, axis=-1, keepdims=True), sink)
            p = jnp.exp2(sg - mx)
            dens.append(jnp.sum(p, axis=-1, keepdims=True) + jnp.exp2(sink - mx))
            ps.append(p.astype(BF16))
        o = _dot(jnp.concatenate(ps, axis=0), vv)
        for g in range(GROUP):
            hd = kv * GROUP + g
            yabuf[r0:r0 + lq, hd * HEAD_DIM:(hd + 1) * HEAD_DIM] = (o[g * lq:(g + 1) * lq] / dens[g]).astype(BF16)

    sc = scores(units[0])
    for idx, unit in enumerate(units):
        sc_next = scores(units[idx + 1]) if idx + 1 < len(units) else None
        if idx in wr_at:
            rnn_proj(wr_at[idx])
        softmax_pv(unit, sc)
        sc = sc_next
    if not sample:
        for s in range(nb):
            kbuf[s, 0:WINDOW, :] = kbuf[s, ts:ts + WINDOW, :]
            vbuf[s, 0:WINDOW, :] = vbuf[s, ts:ts + WINDOW, :]

    attn_proj = [_dot(yabuf[...], wa_ref[:, p * MXU_COLS:(p + 1) * MXU_COLS]) for p in range(n_pieces)]
    acc = None
    for p in range(n_pieces):
        ps = slice(p * MXU_COLS, (p + 1) * MXU_COLS)
        gs = slice(D_MODEL + p * MXU_COLS, D_MODEL + (p + 1) * MXU_COLS)
        g_attn = _sigmoid(glbuf[:, gs] + bgate_ref[:, gs])
        mixed = mixbuf[:, ps] + g_attn * attn_proj[p]
        d = _dot(mixed.astype(BF16), wo_ref[ps, :])
        acc = d if acc is None else acc + d
    x1_ref[...] = (x + acc).reshape(nb, ts, D_MODEL)


def _ffn_kernel(*refs, nb, ts, sample):
    if sample:
        (x_ref, sconv_ref, *rest) = refs
    else:
        (x_ref, *rest) = refs
    (g2_ref, wup_ref, cw_ref, cb_ref, wdn_ref, y_ref, conv_out_ref, cbuf, abuf, hbuf) = rest
    rows = nb * ts
    hist = FFN_CONV - 1
    t = pl.program_id(1)

    if not sample:
        @pl.when(t == 0)
        def _():
            cbuf[...] = jnp.zeros_like(cbuf)

    x = x_ref[...].reshape(rows, D_MODEL)
    hb = _rms_rows(x, g2_ref[...]).astype(BF16)
    for m in range(D_FF // MXU_COLS):
        cs = slice(m * MXU_COLS, (m + 1) * MXU_COLS)
        slot = m % 2
        a = _dot(hb, wup_ref[:, m * MXU_COLS:(m + 1) * MXU_COLS])
        b = _dot(hb, wup_ref[:, D_FF + m * MXU_COLS:D_FF + (m + 1) * MXU_COLS])
        acs = []
        for s in range(nb):
            if sample:
                abuf[slot, s, HIST - hist:HIST, :] = sconv_ref[s, :, cs]
            else:
                abuf[slot, s, 0:HIST, :] = cbuf[s, :, cs]
            abuf[slot, s, HIST:HIST + ts, :] = a[s * ts:(s + 1) * ts]
            acc = a[s * ts:(s + 1) * ts] * cw_ref[hist:hist + 1, cs]
            for j in range(hist):
                acc = acc + abuf[slot, s, HIST - hist + j:HIST - hist + j + ts, :] * cw_ref[j:j + 1, cs]
            acs.append(acc + cb_ref[:, cs])
            conv_out_ref[s, :, cs] = a[(s + 1) * ts - hist:(s + 1) * ts]
            if not sample:
                cbuf[s, :, cs] = a[(s + 1) * ts - HIST:(s + 1) * ts]
        ac = acs[0] if nb == 1 else jnp.concatenate(acs, axis=0)
        gelu = 0.5 * ac * (1.0 + lax.erf(ac * np.float32(math.sqrt(0.5))))
        hbuf[:, cs] = (gelu * b).astype(BF16)
    out = x + _dot(hbuf[...], wdn_ref[...])
    y_ref[...] = out.reshape(nb, ts, D_MODEL)


def _const_spec(shape):
    nd = len(shape)
    return pl.BlockSpec(shape, lambda b, t: (0,) * nd, pipeline_mode=pl.Buffered(1))


def _alibi_bias(lq):
    slopes = np.array([2.0 ** (-8.0 * (h + 1) / N_HEADS) for h in range(N_HEADS)], dtype=np.float32)
    i = np.arange(lq, dtype=np.int32)[:, None]
    j = np.arange(WINDOW + lq, dtype=np.int32)[None, :]
    dist = np.abs(i + WINDOW - j).astype(np.float64)
    return jnp.asarray((slopes.astype(np.float64)[:, None, None] * dist[None] * LOG2E).astype(np.float32))


def _mixer_call(x, state, w, *, ts, sample):
    bsz, seq, _ = x.shape
    if sample:
        nb, grid, lq, n_chunks = bsz, (1, 1), seq, 1
    else:
        nb, grid, lq, n_chunks = 1, (bsz, seq // ts), CHUNK, ts // CHUNK
    rows = nb * ts
    tile = lambda b, t: (b, t, 0)
    per_b = lambda b, t: (b, 0, 0)
    in_specs = [pl.BlockSpec((nb, ts, D_MODEL), tile)]
    args = [x]
    if sample:
        sconv, sh, kc, vc = state
        in_specs += [pl.BlockSpec((nb, RNN_CONV - 1, D_RNN), per_b), pl.BlockSpec((nb, 1, D_RNN), per_b),
                     pl.BlockSpec((nb, WINDOW, KV_W), per_b), pl.BlockSpec((nb, WINDOW, KV_W), per_b)]
        args += [sconv, sh.reshape(bsz, 1, D_RNN), kc.reshape(bsz, WINDOW, KV_W), vc.reshape(bsz, WINDOW, KV_W)]
    weights = [w['g1'], w['w_in'], w['b_gate'], w['rnn_conv_w'], w['rnn_conv_b'], w['w_gate'], w['b_a'], w['b_x'],
               w['lam'], w['q_g'], w['k_g']]
    in_specs += [_const_spec(a.shape) for a in weights]
    args += weights
    in_specs.append(pl.BlockSpec(memory_space=pltpu.SMEM))
    args.append(w['sinks'])
    tail = [_alibi_bias(lq), w['w_rnn_proj'], w['w_attn_proj'], w['w_out']]
    in_specs += [_const_spec(a.shape) for a in tail]
    args += tail

    out_shape = (jax.ShapeDtypeStruct((bsz, seq, D_MODEL), F32),
                 jax.ShapeDtypeStruct((bsz, RNN_CONV - 1, D_RNN), F32),
                 jax.ShapeDtypeStruct((bsz, 1, D_RNN), F32),
                 jax.ShapeDtypeStruct((bsz, WINDOW, KV_W), F32),
                 jax.ShapeDtypeStruct((bsz, WINDOW, KV_W), F32))
    out_specs = (pl.BlockSpec((nb, ts, D_MODEL), tile),
                 pl.BlockSpec((nb, RNN_CONV - 1, D_RNN), per_b),
                 pl.BlockSpec((nb, 1, D_RNN), per_b),
                 pl.BlockSpec((nb, WINDOW, KV_W), per_b),
                 pl.BlockSpec((nb, WINDOW, KV_W), per_b))
    conv_hist = (nb, HIST + ts, D_RNN) if sample else (RNN_CONV - 1, SUBLANES, D_RNN)
    scratch = [pltpu.VMEM(conv_hist, F32),
               pltpu.VMEM((rows, D_RNN), BF16),
               pltpu.VMEM((nb, SUBLANES, D_RNN), F32),
               pltpu.VMEM((nb, WINDOW + ts, KV_W), BF16),
               pltpu.VMEM((nb, WINDOW + ts, KV_W), BF16),
               pltpu.VMEM((rows, Q_W), BF16),
               pltpu.VMEM((rows, Q_W), BF16),
               pltpu.VMEM((rows, 2 * D_MODEL), F32),
               pltpu.VMEM((rows, D_MODEL), F32)]
    kern = functools.partial(_mixer_kernel, nb=nb, ts=ts, lq=lq, n_chunks=n_chunks, sample=sample)
    return pl.pallas_call(
        kern, out_shape=out_shape, grid=grid, in_specs=in_specs, out_specs=out_specs, scratch_shapes=scratch,
        name='mixer_sample' if sample else 'mixer_prompt',
        compiler_params=pltpu.CompilerParams(dimension_semantics=('arbitrary', 'arbitrary'),
                                             vmem_limit_bytes=VMEM_LIMIT_BYTES),
    )(*args)


def _ffn_call(x, state, w, *, ts, sample):
    bsz, seq, _ = x.shape
    if sample:
        nb, grid = bsz, (1, 1)
    else:
        nb, grid = 1, (bsz, seq // ts)
    rows = nb * ts
    tile = lambda b, t: (b, t, 0)
    per_b = lambda b, t: (b, 0, 0)
    in_specs = [pl.BlockSpec((nb, ts, D_MODEL), tile)]
    args = [x]
    if sample:
        in_specs.append(pl.BlockSpec((nb, FFN_CONV - 1, D_FF), per_b))
        args.append(state)
    weights = [w['g2'], w['w_up'], w['ffn_conv_w'], w['ffn_conv_b'], w['w_down']]
    in_specs += [_const_spec(a.shape) for a in weights]
    args += weights
    out_shape = (jax.ShapeDtypeStruct((bsz, seq, D_MODEL), F32),
                 jax.ShapeDtypeStruct((bsz, FFN_CONV - 1, D_FF), F32))
    out_specs = (pl.BlockSpec((nb, ts, D_MODEL), tile),
                 pl.BlockSpec((nb, FFN_CONV - 1, D_FF), per_b))
    scratch = [pltpu.VMEM((nb, HIST, D_FF), F32),
               pltpu.VMEM((2, nb, HIST + ts, MXU_COLS), F32),
               pltpu.VMEM((rows, D_FF), BF16)]
    kern = functools.partial(_ffn_kernel, nb=nb, ts=ts, sample=sample)
    return pl.pallas_call(
        kern, out_shape=out_shape, grid=grid, in_specs=in_specs, out_specs=out_specs, scratch_shapes=scratch,
        name='ffn_sample' if sample else 'ffn_prompt',
        compiler_params=pltpu.CompilerParams(dimension_semantics=('arbitrary', 'arbitrary'),
                                             vmem_limit_bytes=VMEM_LIMIT_BYTES),
    )(*args)


PROMPT_TILE = 512
FFN_TILE = 1024


def kernel(x_prompt, x_sample, state_rnn_conv, state_rnn_h, cache_attn_k, cache_attn_v, state_ffn_conv, norm_mix_g, w_in, b_gate, rnn_conv_w, rnn_conv_b, rnn_gate_a_w, rnn_gate_a_b, rnn_gate_x_w, rnn_gate_x_b, rnn_lambda, q_norm_g, k_norm_g, attn_sinks, w_rnn_proj, w_attn_proj, w_out, norm_ffn_g, w_up, ffn_conv_w, ffn_conv_b, w_down):
    depth = w_in.shape[0]
    xp, xs = x_prompt, x_sample
    outs = [[] for _ in range(10)]
    for l in range(depth):
        w = {
            'g1': norm_mix_g[l][None], 'w_in': w_in[l].astype(BF16), 'b_gate': b_gate[l][None],
            'rnn_conv_w': rnn_conv_w[l], 'rnn_conv_b': rnn_conv_b[l][None],
            'w_gate': jnp.concatenate([rnn_gate_a_w[l], rnn_gate_x_w[l]], axis=-1).astype(BF16),
            'b_a': rnn_gate_a_b[l][None], 'b_x': rnn_gate_x_b[l][None], 'lam': rnn_lambda[l][None],
            'q_g': q_norm_g[l][None], 'k_g': k_norm_g[l][None], 'sinks': attn_sinks[l],
            'w_rnn_proj': w_rnn_proj[l].astype(BF16), 'w_attn_proj': w_attn_proj[l].astype(BF16),
            'w_out': w_out[l].astype(BF16),
            'g2': norm_ffn_g[l][None], 'w_up': w_up[l].astype(BF16), 'ffn_conv_w': ffn_conv_w[l],
            'ffn_conv_b': ffn_conv_b[l][None], 'w_down': w_down[l].astype(BF16),
        }
        xp, conv_p, h_p, k_p, v_p = _mixer_call(xp, None, w, ts=PROMPT_TILE, sample=False)
        xp, fconv_p = _ffn_call(xp, None, w, ts=FFN_TILE, sample=False)
        xs, conv_s, h_s, k_s, v_s = _mixer_call(
            xs, (state_rnn_conv[l], state_rnn_h[l], cache_attn_k[l], cache_attn_v[l]), w,
            ts=xs.shape[1], sample=True)
        xs, fconv_s = _ffn_call(xs, state_ffn_conv[l], w, ts=xs.shape[1], sample=True)
        kv_shape = lambda a: a.reshape(a.shape[0], WINDOW, N_KV_HEADS, HEAD_DIM)
        for j, o in enumerate((conv_p, conv_s, h_p[:, 0], h_s[:, 0], kv_shape(k_p), kv_shape(k_s),
                               kv_shape(v_p), kv_shape(v_s), fconv_p, fconv_s)):
            outs[j].append(o)
    return (xp, xs) + tuple(jnp.stack(o) for o in outs)
```

```python
import functools
import math

import jax
import jax.numpy as jnp
import numpy as np
from jax import lax
from jax.experimental import pallas as pl
from jax.experimental.pallas import tpu as pltpu

D_MODEL = 1024
CHUNK = 64
WINDOW = 128
N_HEADS = 8
N_KV_HEADS = 2
HEAD_DIM = 128
GROUP = N_HEADS // N_KV_HEADS
ATTN_SCALE = HEAD_DIM ** -0.5
D_RNN = D_MODEL
N_RNN_BLOCKS = 8
RNN_BLOCK = D_RNN // N_RNN_BLOCKS
RNN_CONV = 4
RGLRU_C = 8.0
D_FF = ((8 * D_MODEL // 3 + 127) // 128) * 128
FFN_CONV = 3
EPS = 1e-6
Q_W = N_HEADS * HEAD_DIM
KV_W = N_KV_HEADS * HEAD_DIM
IN_W = D_RNN + Q_W + 2 * KV_W + 2 * D_MODEL
U_OFF, Q_OFF, K_OFF, V_OFF, G_OFF = 0, D_RNN, D_RNN + Q_W, D_RNN + Q_W + KV_W, D_RNN + Q_W + 2 * KV_W

SUBLANES = 8
MXU_COLS = 256
PHASES = SUBLANES
HIST = SUBLANES
VMEM_LIMIT_BYTES = 56 * 1024 * 1024
LOG2E = 1.4426950408889634
NEG_LOG2E = -LOG2E

F32 = jnp.float32
BF16 = jnp.bfloat16


def _sigmoid(x):
    return 1.0 / (1.0 + jnp.exp2(x * NEG_LOG2E))


def _sqrt_nonneg(y):
    return jnp.where(y > 0.0, y * lax.rsqrt(y), 0.0)


def _rms_rows(x, g):
    r = lax.rsqrt(jnp.mean(x * x, axis=-1, keepdims=True) + EPS)
    return x * r * g


def _dot(a, b):
    return jnp.dot(a, b, preferred_element_type=F32)


def _dot_nt(a, b):
    return lax.dot_general(a, b, (((1,), (1,)), ((), ())), preferred_element_type=F32)


def _scan_rows(a, b, h0, row_id):
    outs = []
    carry = h0
    for g in range(a.shape[0] // SUBLANES):
        ag = a[g * SUBLANES:(g + 1) * SUBLANES]
        bg = b[g * SUBLANES:(g + 1) * SUBLANES]
        for sh in (1, 2, 4):
            m = row_id >= sh
            bg = jnp.where(m, ag * pltpu.roll(bg, sh, 0) + bg, bg)
            ag = jnp.where(m, ag * pltpu.roll(ag, sh, 0), ag)
        h = ag * carry + bg
        outs.append(h)
        carry = jnp.broadcast_to(h[SUBLANES - 1:SUBLANES, :], h.shape)
    return (outs[0] if len(outs) == 1 else jnp.concatenate(outs, axis=0)), carry


def _row_groups(v):
    return [v[k * SUBLANES:(k + 1) * SUBLANES] for k in range(v.shape[0] // SUBLANES)]


def _shift_rows(v, prev_group, row_id):
    rolled = [pltpu.roll(g, 1, 0) for g in [prev_group] + _row_groups(v)]
    return jnp.concatenate([jnp.where(row_id >= 1, rolled[k + 1], rolled[k]) for k in range(len(rolled) - 1)], axis=0)


def _to_piece_order(v):
    n, c = v.shape
    return jnp.swapaxes(v.reshape(n // PHASES, PHASES, c), 0, 1).reshape(n, c)


def _from_piece_order(v):
    n, c = v.shape
    return jnp.swapaxes(v.reshape(PHASES, n // PHASES, c), 0, 1).reshape(n, c)


def _mixer_kernel(*refs, nb, ts, lq, n_chunks, sample):
    if sample:
        (x_ref, sconv_ref, sh_ref, kc_ref, vc_ref, *rest) = refs
    else:
        (x_ref, *rest) = refs
    (g1_ref, w_in_ref, bgate_ref, cw_ref, cb_ref, wg_ref, ba_ref, bx_ref, lam_ref, qg_ref, kg_ref,
     sinks_ref, bias_ref, wr_ref, wa_ref, wo_ref,
     x1_ref, conv_out_ref, h_out_ref, k_out_ref, v_out_ref,
     ubuf, ybuf, hcar, kbuf, vbuf, qbuf, yabuf, glbuf, mixbuf) = rest

    rows = nb * ts
    t = pl.program_id(1)

    if not sample:
        @pl.when(t == 0)
        def _():
            ubuf[...] = jnp.zeros_like(ubuf)
            hcar[...] = jnp.zeros_like(hcar)
            kbuf[:, 0:WINDOW, :] = jnp.zeros((nb, WINDOW, KV_W), BF16)
            vbuf[:, 0:WINDOW, :] = jnp.zeros((nb, WINDOW, KV_W), BF16)

    x = x_ref[...].reshape(rows, D_MODEL)
    hb = _rms_rows(x, g1_ref[...]).astype(BF16)

    lam = lam_ref[...]
    decay_log2 = (-RGLRU_C * LOG2E) * (jnp.maximum(-lam, 0.0) + jnp.log1p(jnp.exp(-jnp.abs(lam))))
    row_id = lax.broadcasted_iota(jnp.int32, (SUBLANES, MXU_COLS), 0)

    def project_q(half):
        q = _dot(hb, w_in_ref[:, Q_OFF + half * 512:Q_OFF + (half + 1) * 512])
        for j in range(4):
            hd = half * 4 + j
            qbuf[:, hd * HEAD_DIM:(hd + 1) * HEAD_DIM] = _rms_rows(
                q[:, j * HEAD_DIM:(j + 1) * HEAD_DIM], qg_ref[...]).astype(BF16)

    def project_kv():
        kvp = _dot(hb, w_in_ref[:, K_OFF:K_OFF + 2 * KV_W])
        kn = jnp.concatenate(
            [_rms_rows(kvp[:, j * HEAD_DIM:(j + 1) * HEAD_DIM], kg_ref[...]) for j in range(N_KV_HEADS)], axis=1)
        v = kvp[:, KV_W:2 * KV_W]
        for s in range(nb):
            if sample:
                kbuf[s, 0:WINDOW, :] = kc_ref[s].astype(BF16)
                vbuf[s, 0:WINDOW, :] = vc_ref[s].astype(BF16)
                k_out_ref[s, 0:WINDOW - ts, :] = kc_ref[s, ts:WINDOW, :]
                v_out_ref[s, 0:WINDOW - ts, :] = vc_ref[s, ts:WINDOW, :]
                k_out_ref[s, WINDOW - ts:WINDOW, :] = kn[s * ts:(s + 1) * ts]
                v_out_ref[s, WINDOW - ts:WINDOW, :] = v[s * ts:(s + 1) * ts]
            else:
                k_out_ref[s] = kn[(s + 1) * ts - WINDOW:(s + 1) * ts]
                v_out_ref[s] = v[(s + 1) * ts - WINDOW:(s + 1) * ts]
            kbuf[s, WINDOW:WINDOW + ts, :] = kn[s * ts:(s + 1) * ts].astype(BF16)
            vbuf[s, WINDOW:WINDOW + ts, :] = v[s * ts:(s + 1) * ts].astype(BF16)

    def project_gate(part):
        glbuf[:, part * 512:(part + 1) * 512] = _dot(
            hb, w_in_ref[:, G_OFF + part * 512:G_OFF + (part + 1) * 512])

    side_work = [[lambda: project_q(1)],
                 [project_kv, lambda: project_gate(0)],
                 [lambda: project_gate(1)],
                 [lambda: project_gate(2)]]
    tail_work = [lambda: project_gate(3)]

    n_blocks = D_RNN // MXU_COLS
    per_block = MXU_COLS // RNN_BLOCK

    def u_proj(m):
        return _dot(hb, w_in_ref[:, U_OFF + m * MXU_COLS:U_OFF + (m + 1) * MXU_COLS])

    g_rows = ts // PHASES
    hist = RNN_CONV - 1

    def conv_block_pieces(m, u):
        cs = slice(m * MXU_COLS, (m + 1) * MXU_COLS)
        u = _to_piece_order(u)
        shifted = []
        for k in range(hist):
            pc = PHASES - hist + k
            piece = u[pc * g_rows:(pc + 1) * g_rows]
            shifted.append(_shift_rows(piece, ubuf[k, :, cs], row_id))
            ubuf[k, :, cs] = piece[g_rows - SUBLANES:g_rows]
            conv_out_ref[0, k:k + 1, cs] = piece[g_rows - 1:g_rows]
        ext = jnp.concatenate(shifted + [u], axis=0)
        acc = ext[hist * g_rows:hist * g_rows + ts] * cw_ref[hist:hist + 1, cs]
        for k in range(hist):
            acc = acc + ext[k * g_rows:k * g_rows + ts] * cw_ref[k:k + 1, cs]
        return acc + cb_ref[:, cs]

    def scan_block_pieces(m, a_all, b_all):
        cs = slice(m * MXU_COLS, (m + 1) * MXU_COLS)
        h0 = hcar[0, :, cs]
        loc = b_all[0:g_rows]
        dec = a_all[0:g_rows]
        locs, decs = [loc], [dec]
        for j in range(1, PHASES):
            aj = a_all[j * g_rows:(j + 1) * g_rows]
            loc = aj * loc + b_all[j * g_rows:(j + 1) * g_rows]
            dec = aj * dec
            locs.append(loc)
            decs.append(dec)
        ends, carry = _scan_rows(dec, loc, h0, row_id)
        entry = _shift_rows(ends, h0, row_id)
        outs = [locs[j] + decs[j] * entry for j in range(PHASES - 1)] + [ends]
        ybuf[:, cs] = _from_piece_order(jnp.concatenate(outs, axis=0)).astype(BF16)
        h_out_ref[0, :, cs] = carry[0:1, :]
        hcar[0, :, cs] = carry

    def conv_block(m, u):
        if not sample:
            return conv_block_pieces(m, u)
        cs = slice(m * MXU_COLS, (m + 1) * MXU_COLS)
        ucs = []
        for s in range(nb):
            ubuf[s, HIST - hist:HIST, cs] = sconv_ref[s, :, cs]
            ubuf[s, HIST:HIST + ts, cs] = u[s * ts:(s + 1) * ts]
            acc = u[s * ts:(s + 1) * ts] * cw_ref[hist:hist + 1, cs]
            for j in range(hist):
                acc = acc + ubuf[s, HIST - hist + j:HIST - hist + j + ts, cs] * cw_ref[j:j + 1, cs]
            ucs.append(acc + cb_ref[:, cs])
            conv_out_ref[s, :, cs] = ubuf[s, HIST + ts - hist:HIST + ts, cs]
        return jnp.concatenate(ucs, axis=0)

    def gate_dots(m, ucb):
        return [_dot(ucb[:, j * RNN_BLOCK:(j + 1) * RNN_BLOCK], wg_ref[m * per_block + j]) for j in range(per_block)]

    def gate_math(m, zs, uc):
        a_parts, b_parts = [], []
        for j, z in enumerate(zs):
            n = m * per_block + j
            sl = slice(n * RNN_BLOCK, (n + 1) * RNN_BLOCK)
            r = _sigmoid(z[:, :RNN_BLOCK] + ba_ref[:, sl])
            i = _sigmoid(z[:, RNN_BLOCK:] + bx_ref[:, sl])
            a = jnp.exp2(decay_log2[:, sl] * r)
            a_parts.append(a)
            b_parts.append(_sqrt_nonneg(1.0 - a * a) * (i * uc[:, j * RNN_BLOCK:(j + 1) * RNN_BLOCK]))
        return jnp.concatenate(a_parts, axis=1), jnp.concatenate(b_parts, axis=1)

    def scan_block(m, a_all, b_all):
        if not sample:
            return scan_block_pieces(m, a_all, b_all)
        cs = slice(m * MXU_COLS, (m + 1) * MXU_COLS)
        for s in range(nb):
            h0 = jnp.broadcast_to(sh_ref[s, :, cs], (SUBLANES, MXU_COLS))
            y, hl = _scan_rows(a_all[s * ts:(s + 1) * ts], b_all[s * ts:(s + 1) * ts], h0, row_id)
            ybuf[s * ts:(s + 1) * ts, cs] = y.astype(BF16)
            h_out_ref[s, :, cs] = hl[0:1, :]

    u_first = u_proj(0)
    project_q(0)
    uc = conv_block(0, u_first)
    for m in range(n_blocks):
        zs = gate_dots(m, uc.astype(BF16))
        u_next = u_proj(m + 1) if m + 1 < n_blocks else None
        for work in side_work[m]:
            work()
        a_all, b_all = gate_math(m, zs, uc)
        if u_next is not None:
            uc = conv_block(m + 1, u_next)
        else:
            for work in tail_work:
                work()
        scan_block(m, a_all, b_all)

    lk = WINDOW + lq
    col_id = lax.broadcasted_iota(jnp.int32, (lq, lk), 1)
    units = [(s, c, kv) for s in range(nb) for c in range(n_chunks) for kv in range(N_KV_HEADS)]
    n_pieces = D_MODEL // MXU_COLS
    wr_at = {(len(units) * p) // n_pieces: p for p in range(n_pieces)}

    def rnn_proj(p):
        ps = slice(p * MXU_COLS, (p + 1) * MXU_COLS)
        g_rnn = _sigmoid(glbuf[:, ps] + bgate_ref[:, ps])
        mixbuf[:, ps] = g_rnn * _dot(ybuf[...], wr_ref[:, ps])

    def scores(unit):
        s, c, kv = unit
        r0 = s * ts + c * lq
        kk = kbuf[s, c * lq:c * lq + lk, kv * HEAD_DIM:(kv + 1) * HEAD_DIM]
        qs = jnp.concatenate(
            [qbuf[r0:r0 + lq, (kv * GROUP + g) * HEAD_DIM:(kv * GROUP + g + 1) * HEAD_DIM]
             for g in range(GROUP)], axis=0)
        return _dot_nt(qs, kk)

    def softmax_pv(unit, sc):
        s, c, kv = unit
        r0 = s * ts + c * lq
        vv = vbuf[s, c * lq:c * lq + lk, kv * HEAD_DIM:(kv + 1) * HEAD_DIM]
        sc = sc * (ATTN_SCALE * LOG2E)
        ps = []
        dens = []
        for g in range(GROUP):
            hd = kv * GROUP + g
            sg = sc[g * lq:(g + 1) * lq] - bias_ref[hd]
            if (not sample) and c * lq < WINDOW:
                first_valid = jnp.where(t == 0, WINDOW - c * lq, 0)
                sg = jnp.where(col_id >= first_valid, sg, -jnp.inf)
            sink = sinks_ref[hd] * LOG2E
            mx = jnp.maximum(jnp.max(sg, axis=-1, keepdims=True), sink)
            p = jnp.exp2(sg - mx)
            dens.append(jnp.sum(p, axis=-1, keepdims=True) + jnp.exp2(sink - mx))
            ps.append(p.astype(BF16))
        o = _dot(jnp.concatenate(ps, axis=0), vv)
        for g in range(GROUP):
            hd = kv * GROUP + g
            yabuf[r0:r0 + lq, hd * HEAD_DIM:(hd + 1) * HEAD_DIM] = (o[g * lq:(g + 1) * lq] / dens[g]).astype(BF16)

    sc = scores(units[0])
    for idx, unit in enumerate(units):
        sc_next = scores(units[idx + 1]) if idx + 1 < len(units) else None
        if idx in wr_at:
            rnn_proj(wr_at[idx])
        softmax_pv(unit, sc)
        sc = sc_next
    if not sample:
        for s in range(nb):
            kbuf[s, 0:WINDOW, :] = kbuf[s, ts:ts + WINDOW, :]
            vbuf[s, 0:WINDOW, :] = vbuf[s, ts:ts + WINDOW, :]

    attn_proj = [_dot(yabuf[...], wa_ref[:, p * MXU_COLS:(p + 1) * MXU_COLS]) for p in range(n_pieces)]
    acc = None
    for p in range(n_pieces):
        ps = slice(p * MXU_COLS, (p + 1) * MXU_COLS)
        gs = slice(D_MODEL + p * MXU_COLS, D_MODEL + (p + 1) * MXU_COLS)
        g_attn = _sigmoid(glbuf[:, gs] + bgate_ref[:, gs])
        mixed = mixbuf[:, ps] + g_attn * attn_proj[p]
        d = _dot(mixed.astype(BF16), wo_ref[ps, :])
        acc = d if acc is None else acc + d
    x1_ref[...] = (x + acc).reshape(nb, ts, D_MODEL)


def _ffn_kernel(*refs, nb, ts, sample):
    if sample:
        (x_ref, sconv_ref, *rest) = refs
    else:
        (x_ref, *rest) = refs
    (g2_ref, wup_ref, cw_ref, cb_ref, wdn_ref, y_ref, conv_out_ref, cbuf, abuf, hbuf) = rest
    rows = nb * ts
    hist = FFN_CONV - 1
    t = pl.program_id(1)

    if not sample:
        @pl.when(t == 0)
        def _():
            cbuf[...] = jnp.zeros_like(cbuf)

    x = x_ref[...].reshape(rows, D_MODEL)
    hb = _rms_rows(x, g2_ref[...]).astype(BF16)
    for m in range(D_FF // MXU_COLS):
        cs = slice(m * MXU_COLS, (m + 1) * MXU_COLS)
        slot = m % 2
        a = _dot(hb, wup_ref[:, m * MXU_COLS:(m + 1) * MXU_COLS])
        b = _dot(hb, wup_ref[:, D_FF + m * MXU_COLS:D_FF + (m + 1) * MXU_COLS])
        acs = []
        for s in range(nb):
            if sample:
                abuf[slot, s, HIST - hist:HIST, :] = sconv_ref[s, :, cs]
            else:
                abuf[slot, s, 0:HIST, :] = cbuf[s, :, cs]
            abuf[slot, s, HIST:HIST + ts, :] = a[s * ts:(s + 1) * ts]
            acc = a[s * ts:(s + 1) * ts] * cw_ref[hist:hist + 1, cs]
            for j in range(hist):
                acc = acc + abuf[slot, s, HIST - hist + j:HIST - hist + j + ts, :] * cw_ref[j:j + 1, cs]
            acs.append(acc + cb_ref[:, cs])
            conv_out_ref[s, :, cs] = a[(s + 1) * ts - hist:(s + 1) * ts]
            if not sample:
                cbuf[s, :, cs] = a[(s + 1) * ts - HIST:(s + 1) * ts]
        ac = acs[0] if nb == 1 else jnp.concatenate(acs, axis=0)
        gelu = 0.5 * ac * (1.0 + lax.erf(ac * np.float32(math.sqrt(0.5))))
        hbuf[:, cs] = (gelu * b).astype(BF16)
    out = x + _dot(hbuf[...], wdn_ref[...])
    y_ref[...] = out.reshape(nb, ts, D_MODEL)


def _const_spec(shape):
    nd = len(shape)
    return pl.BlockSpec(shape, lambda b, t: (0,) * nd, pipeline_mode=pl.Buffered(1))


def _nbytes(shape, dtype):
    return math.prod(shape) * jnp.dtype(dtype).itemsize


def _check_vmem(name, scratch, tile_bytes, resident):
    need = (sum(_nbytes(s.shape, s.dtype) for s in scratch) + 2 * tile_bytes
            + sum(_nbytes(a.shape, a.dtype) for a in resident))
    assert need <= VMEM_LIMIT_BYTES, f'{name}: planned VMEM {need} B exceeds the limit {VMEM_LIMIT_BYTES} B'


def _alibi_bias(lq):
    slopes = np.array([2.0 ** (-8.0 * (h + 1) / N_HEADS) for h in range(N_HEADS)], dtype=np.float32)
    i = np.arange(lq, dtype=np.int32)[:, None]
    j = np.arange(WINDOW + lq, dtype=np.int32)[None, :]
    dist = np.abs(i + WINDOW - j).astype(np.float64)
    return jnp.asarray((slopes.astype(np.float64)[:, None, None] * dist[None] * LOG2E).astype(np.float32))


def _mixer_call(x, state, w, *, ts, sample):
    bsz, seq, _ = x.shape
    if sample:
        nb, grid, lq, n_chunks = bsz, (1, 1), seq, 1
    else:
        nb, grid, lq, n_chunks = 1, (bsz, seq // ts), CHUNK, ts // CHUNK
    rows = nb * ts
    tile = lambda b, t: (b, t, 0)
    per_b = lambda b, t: (b, 0, 0)
    in_specs = [pl.BlockSpec((nb, ts, D_MODEL), tile)]
    args = [x]
    if sample:
        sconv, sh, kc, vc = state
        in_specs += [pl.BlockSpec((nb, RNN_CONV - 1, D_RNN), per_b), pl.BlockSpec((nb, 1, D_RNN), per_b),
                     pl.BlockSpec((nb, WINDOW, KV_W), per_b), pl.BlockSpec((nb, WINDOW, KV_W), per_b)]
        args += [sconv, sh.reshape(bsz, 1, D_RNN), kc.reshape(bsz, WINDOW, KV_W), vc.reshape(bsz, WINDOW, KV_W)]
    weights = [w['g1'], w['w_in'], w['b_gate'], w['rnn_conv_w'], w['rnn_conv_b'], w['w_gate'], w['b_a'], w['b_x'],
               w['lam'], w['q_g'], w['k_g']]
    in_specs += [_const_spec(a.shape) for a in weights]
    args += weights
    in_specs.append(pl.BlockSpec(memory_space=pltpu.SMEM))
    args.append(w['sinks'])
    tail = [_alibi_bias(lq), w['w_rnn_proj'], w['w_attn_proj'], w['w_out']]
    in_specs += [_const_spec(a.shape) for a in tail]
    args += tail

    out_shape = (jax.ShapeDtypeStruct((bsz, seq, D_MODEL), F32),
                 jax.ShapeDtypeStruct((bsz, RNN_CONV - 1, D_RNN), F32),
                 jax.ShapeDtypeStruct((bsz, 1, D_RNN), F32),
                 jax.ShapeDtypeStruct((bsz, WINDOW, KV_W), F32),
                 jax.ShapeDtypeStruct((bsz, WINDOW, KV_W), F32))
    out_specs = (pl.BlockSpec((nb, ts, D_MODEL), tile),
                 pl.BlockSpec((nb, RNN_CONV - 1, D_RNN), per_b),
                 pl.BlockSpec((nb, 1, D_RNN), per_b),
                 pl.BlockSpec((nb, WINDOW, KV_W), per_b),
                 pl.BlockSpec((nb, WINDOW, KV_W), per_b))
    conv_hist = (nb, HIST + ts, D_RNN) if sample else (RNN_CONV - 1, SUBLANES, D_RNN)
    scratch = [pltpu.VMEM(conv_hist, F32),
               pltpu.VMEM((rows, D_RNN), BF16),
               pltpu.VMEM((nb, SUBLANES, D_RNN), F32),
               pltpu.VMEM((nb, WINDOW + ts, KV_W), BF16),
               pltpu.VMEM((nb, WINDOW + ts, KV_W), BF16),
               pltpu.VMEM((rows, Q_W), BF16),
               pltpu.VMEM((rows, Q_W), BF16),
               pltpu.VMEM((rows, 2 * D_MODEL), F32),
               pltpu.VMEM((rows, D_MODEL), F32)]
    _check_vmem('mixer', scratch, 2 * _nbytes((rows, D_MODEL), F32), args[1:])
    kern = functools.partial(_mixer_kernel, nb=nb, ts=ts, lq=lq, n_chunks=n_chunks, sample=sample)
    return pl.pallas_call(
        kern, out_shape=out_shape, grid=grid, in_specs=in_specs, out_specs=out_specs, scratch_shapes=scratch,
        name='mixer_sample' if sample else 'mixer_prompt',
        compiler_params=pltpu.CompilerParams(dimension_semantics=('arbitrary', 'arbitrary'),
                                             vmem_limit_bytes=VMEM_LIMIT_BYTES),
    )(*args)


def _ffn_call(x, state, w, *, ts, sample):
    bsz, seq, _ = x.shape
    if sample:
        nb, grid = bsz, (1, 1)
    else:
        nb, grid = 1, (bsz, seq // ts)
    rows = nb * ts
    tile = lambda b, t: (b, t, 0)
    per_b = lambda b, t: (b, 0, 0)
    in_specs = [pl.BlockSpec((nb, ts, D_MODEL), tile)]
    args = [x]
    if sample:
        in_specs.append(pl.BlockSpec((nb, FFN_CONV - 1, D_FF), per_b))
        args.append(state)
    weights = [w['g2'], w['w_up'], w['ffn_conv_w'], w['ffn_conv_b'], w['w_down']]
    in_specs += [_const_spec(a.shape) for a in weights]
    args += weights
    out_shape = (jax.ShapeDtypeStruct((bsz, seq, D_MODEL), F32),
                 jax.ShapeDtypeStruct((bsz, FFN_CONV - 1, D_FF), F32))
    out_specs = (pl.BlockSpec((nb, ts, D_MODEL), tile),
                 pl.BlockSpec((nb, FFN_CONV - 1, D_FF), per_b))
    scratch = [pltpu.VMEM((nb, HIST, D_FF), F32),
               pltpu.VMEM((2, nb, HIST + ts, MXU_COLS), F32),
               pltpu.VMEM((rows, D_FF), BF16)]
    _check_vmem('ffn', scratch, 2 * _nbytes((rows, D_MODEL), F32), args[1:])
    kern = functools.partial(_ffn_kernel, nb=nb, ts=ts, sample=sample)
    return pl.pallas_call(
        kern, out_shape=out_shape, grid=grid, in_specs=in_specs, out_specs=out_specs, scratch_shapes=scratch,
        name='ffn_sample' if sample else 'ffn_prompt',
        compiler_params=pltpu.CompilerParams(dimension_semantics=('arbitrary', 'arbitrary'),
                                             vmem_limit_bytes=VMEM_LIMIT_BYTES),
    )(*args)


PROMPT_TILE = 512
FFN_TILE = 1024


def kernel(x_prompt, x_sample, state_rnn_conv, state_rnn_h, cache_attn_k, cache_attn_v, state_ffn_conv, norm_mix_g, w_in, b_gate, rnn_conv_w, rnn_conv_b, rnn_gate_a_w, rnn_gate_a_b, rnn_gate_x_w, rnn_gate_x_b, rnn_lambda, q_norm_g, k_norm_g, attn_sinks, w_rnn_proj, w_attn_proj, w_out, norm_ffn_g, w_up, ffn_conv_w, ffn_conv_b, w_down):
    depth = w_in.shape[0]
    xp, xs = x_prompt, x_sample
    assert xp.shape[1] % PROMPT_TILE == 0 and xp.shape[1] % FFN_TILE == 0 and xp.shape[2] == D_MODEL
    assert xs.shape[1] % (2 * SUBLANES) == 0 and xs.shape[1] <= WINDOW and cache_attn_k.shape[2] == WINDOW
    outs = [[] for _ in range(10)]
    for l in range(depth):
        w = {
            'g1': norm_mix_g[l][None], 'w_in': w_in[l].astype(BF16), 'b_gate': b_gate[l][None],
            'rnn_conv_w': rnn_conv_w[l], 'rnn_conv_b': rnn_conv_b[l][None],
            'w_gate': jnp.concatenate([rnn_gate_a_w[l], rnn_gate_x_w[l]], axis=-1).astype(BF16),
            'b_a': rnn_gate_a_b[l][None], 'b_x': rnn_gate_x_b[l][None], 'lam': rnn_lambda[l][None],
            'q_g': q_norm_g[l][None], 'k_g': k_norm_g[l][None], 'sinks': attn_sinks[l],
            'w_rnn_proj': w_rnn_proj[l].astype(BF16), 'w_attn_proj': w_attn_proj[l].astype(BF16),
            'w_out': w_out[l].astype(BF16),
            'g2': norm_ffn_g[l][None], 'w_up': w_up[l].astype(BF16), 'ffn_conv_w': ffn_conv_w[l],
            'ffn_conv_b': ffn_conv_b[l][None], 'w_down': w_down[l].astype(BF16),
        }
        xp, conv_p, h_p, k_p, v_p = _mixer_call(xp, None, w, ts=PROMPT_TILE, sample=False)
        xp, fconv_p = _ffn_call(xp, None, w, ts=FFN_TILE, sample=False)
        xs, conv_s, h_s, k_s, v_s = _mixer_call(
            xs, (state_rnn_conv[l], state_rnn_h[l], cache_attn_k[l], cache_attn_v[l]), w,
            ts=xs.shape[1], sample=True)
        xs, fconv_s = _ffn_call(xs, state_ffn_conv[l], w, ts=xs.shape[1], sample=True)
        kv_shape = lambda a: a.reshape(a.shape[0], WINDOW, N_KV_HEADS, HEAD_DIM)
        for j, o in enumerate((conv_p, conv_s, h_p[:, 0], h_s[:, 0], kv_shape(k_p), kv_shape(k_s),
                               kv_shape(v_p), kv_shape(v_s), fconv_p, fconv_s)):
            outs[j].append(o)
    return (xp, xs) + tuple(jnp.stack(o) for o in outs)
```

```python
import functools
import math

import jax
import jax.numpy as jnp
import numpy as np
from jax import lax
from jax.experimental import pallas as pl
from jax.experimental.pallas import tpu as pltpu

D_MODEL = 1024
CHUNK = 64
WINDOW = 128
N_HEADS = 8
N_KV_HEADS = 2
HEAD_DIM = 128
GROUP = N_HEADS // N_KV_HEADS
ATTN_SCALE = HEAD_DIM ** -0.5
D_RNN = D_MODEL
N_RNN_BLOCKS = 8
RNN_BLOCK = D_RNN // N_RNN_BLOCKS
RNN_CONV = 4
RGLRU_C = 8.0
D_FF = ((8 * D_MODEL // 3 + 127) // 128) * 128
FFN_CONV = 3
EPS = 1e-6
Q_W = N_HEADS * HEAD_DIM
KV_W = N_KV_HEADS * HEAD_DIM
IN_W = D_RNN + Q_W + 2 * KV_W + 2 * D_MODEL
U_OFF, Q_OFF, K_OFF, V_OFF, G_OFF = 0, D_RNN, D_RNN + Q_W, D_RNN + Q_W + KV_W, D_RNN + Q_W + 2 * KV_W

SUBLANES = 8
MXU_COLS = 256
PHASES = SUBLANES
HIST = SUBLANES
VMEM_LIMIT_BYTES = 56 * 1024 * 1024
LOG2E = 1.4426950408889634
NEG_LOG2E = -LOG2E

F32 = jnp.float32
BF16 = jnp.bfloat16


def _sigmoid(x):
    return 1.0 / (1.0 + jnp.exp2(x * NEG_LOG2E))


def _sqrt_nonneg(y):
    return jnp.where(y > 0.0, y * lax.rsqrt(y), 0.0)


def _rms_rows(x, g):
    r = lax.rsqrt(jnp.mean(x * x, axis=-1, keepdims=True) + EPS)
    return x * r * g


def _dot(a, b):
    return jnp.dot(a, b, preferred_element_type=F32)


def _dot_nt(a, b):
    return lax.dot_general(a, b, (((1,), (1,)), ((), ())), preferred_element_type=F32)


def _scan_rows(a, b, h0, row_id):
    outs = []
    carry = h0
    for g in range(a.shape[0] // SUBLANES):
        ag = a[g * SUBLANES:(g + 1) * SUBLANES]
        bg = b[g * SUBLANES:(g + 1) * SUBLANES]
        for sh in (1, 2, 4):
            m = row_id >= sh
            bg = jnp.where(m, ag * pltpu.roll(bg, sh, 0) + bg, bg)
            ag = jnp.where(m, ag * pltpu.roll(ag, sh, 0), ag)
        h = ag * carry + bg
        outs.append(h)
        carry = jnp.broadcast_to(h[SUBLANES - 1:SUBLANES, :], h.shape)
    return (outs[0] if len(outs) == 1 else jnp.concatenate(outs, axis=0)), carry


def _row_groups(v):
    return [v[k * SUBLANES:(k + 1) * SUBLANES] for k in range(v.shape[0] // SUBLANES)]


def _shift_rows(v, prev_group, row_id):
    rolled = [pltpu.roll(g, 1, 0) for g in [prev_group] + _row_groups(v)]
    return jnp.concatenate([jnp.where(row_id >= 1, rolled[k + 1], rolled[k]) for k in range(len(rolled) - 1)], axis=0)


def _to_piece_order(v):
    n, c = v.shape
    return jnp.swapaxes(v.reshape(n // PHASES, PHASES, c), 0, 1).reshape(n, c)


def _from_piece_order(v):
    n, c = v.shape
    return jnp.swapaxes(v.reshape(PHASES, n // PHASES, c), 0, 1).reshape(n, c)


def _mixer_kernel(*refs, nb, ts, lq, n_chunks, sample):
    if sample:
        (x_ref, sconv_ref, sh_ref, kc_ref, vc_ref, *rest) = refs
    else:
        (x_ref, *rest) = refs
    (g1_ref, w_in_ref, bgate_ref, cw_ref, cb_ref, wg_ref, ba_ref, bx_ref, lam_ref, qg_ref, kg_ref,
     sinks_ref, bias_ref, wr_ref, wa_ref, wo_ref,
     x1_ref, conv_out_ref, h_out_ref, k_out_ref, v_out_ref,
     ubuf, ybuf, hcar, kbuf, vbuf, qbuf, yabuf, glbuf, mixbuf) = rest

    rows = nb * ts
    t = pl.program_id(1)

    if not sample:
        @pl.when(t == 0)
        def _():
            ubuf[...] = jnp.zeros_like(ubuf)
            hcar[...] = jnp.zeros_like(hcar)
            kbuf[:, 0:WINDOW, :] = jnp.zeros((nb, WINDOW, KV_W), BF16)
            vbuf[:, 0:WINDOW, :] = jnp.zeros((nb, WINDOW, KV_W), BF16)

    x = x_ref[...].reshape(rows, D_MODEL)
    half_rows = rows // 2
    hb_halves = [_rms_rows(x[i * half_rows:(i + 1) * half_rows], g1_ref[...]).astype(BF16) for i in range(2)]
    hb = jnp.concatenate(hb_halves, axis=0)

    lam = lam_ref[...]
    decay_log2 = (-RGLRU_C * LOG2E) * (jnp.maximum(-lam, 0.0) + jnp.log1p(jnp.exp(-jnp.abs(lam))))
    row_id = lax.broadcasted_iota(jnp.int32, (SUBLANES, MXU_COLS), 0)

    def project_q(half):
        q = _dot(hb, w_in_ref[:, Q_OFF + half * 512:Q_OFF + (half + 1) * 512])
        for j in range(4):
            hd = half * 4 + j
            qbuf[:, hd * HEAD_DIM:(hd + 1) * HEAD_DIM] = _rms_rows(
                q[:, j * HEAD_DIM:(j + 1) * HEAD_DIM], qg_ref[...]).astype(BF16)

    def project_kv():
        kvp = _dot(hb, w_in_ref[:, K_OFF:K_OFF + 2 * KV_W])
        kn = jnp.concatenate(
            [_rms_rows(kvp[:, j * HEAD_DIM:(j + 1) * HEAD_DIM], kg_ref[...]) for j in range(N_KV_HEADS)], axis=1)
        v = kvp[:, KV_W:2 * KV_W]
        for s in range(nb):
            if sample:
                kbuf[s, 0:WINDOW, :] = kc_ref[s].astype(BF16)
                vbuf[s, 0:WINDOW, :] = vc_ref[s].astype(BF16)
                k_out_ref[s, 0:WINDOW - ts, :] = kc_ref[s, ts:WINDOW, :]
                v_out_ref[s, 0:WINDOW - ts, :] = vc_ref[s, ts:WINDOW, :]
                k_out_ref[s, WINDOW - ts:WINDOW, :] = kn[s * ts:(s + 1) * ts]
                v_out_ref[s, WINDOW - ts:WINDOW, :] = v[s * ts:(s + 1) * ts]
            else:
                k_out_ref[s] = kn[(s + 1) * ts - WINDOW:(s + 1) * ts]
                v_out_ref[s] = v[(s + 1) * ts - WINDOW:(s + 1) * ts]
            kbuf[s, WINDOW:WINDOW + ts, :] = kn[s * ts:(s + 1) * ts].astype(BF16)
            vbuf[s, WINDOW:WINDOW + ts, :] = v[s * ts:(s + 1) * ts].astype(BF16)

    def project_gate(part):
        glbuf[:, part * 512:(part + 1) * 512] = _dot(
            hb, w_in_ref[:, G_OFF + part * 512:G_OFF + (part + 1) * 512])

    side_work = [[lambda: project_q(1)],
                 [project_kv, lambda: project_gate(0)],
                 [lambda: project_gate(1)],
                 [lambda: project_gate(2)]]
    tail_work = [lambda: project_gate(3)]

    n_blocks = D_RNN // MXU_COLS
    per_block = MXU_COLS // RNN_BLOCK

    def u_proj(m):
        return _dot(hb, w_in_ref[:, U_OFF + m * MXU_COLS:U_OFF + (m + 1) * MXU_COLS])

    g_rows = ts // PHASES
    hist = RNN_CONV - 1

    def conv_block_pieces(m, u):
        cs = slice(m * MXU_COLS, (m + 1) * MXU_COLS)
        u = _to_piece_order(u)
        shifted = []
        for k in range(hist):
            pc = PHASES - hist + k
            piece = u[pc * g_rows:(pc + 1) * g_rows]
            shifted.append(_shift_rows(piece, ubuf[k, :, cs], row_id))
            ubuf[k, :, cs] = piece[g_rows - SUBLANES:g_rows]
            conv_out_ref[0, k:k + 1, cs] = piece[g_rows - 1:g_rows]
        ext = jnp.concatenate(shifted + [u], axis=0)
        acc = ext[hist * g_rows:hist * g_rows + ts] * cw_ref[hist:hist + 1, cs]
        for k in range(hist):
            acc = acc + ext[k * g_rows:k * g_rows + ts] * cw_ref[k:k + 1, cs]
        return acc + cb_ref[:, cs]

    def scan_block_pieces(m, a_all, b_all):
        cs = slice(m * MXU_COLS, (m + 1) * MXU_COLS)
        h0 = hcar[0, :, cs]
        loc = b_all[0:g_rows]
        dec = a_all[0:g_rows]
        locs, decs = [loc], [dec]
        for j in range(1, PHASES):
            aj = a_all[j * g_rows:(j + 1) * g_rows]
            loc = aj * loc + b_all[j * g_rows:(j + 1) * g_rows]
            dec = aj * dec
            locs.append(loc)
            decs.append(dec)
        ends, carry = _scan_rows(dec, loc, h0, row_id)
        entry = _shift_rows(ends, h0, row_id)
        outs = [locs[j] + decs[j] * entry for j in range(PHASES - 1)] + [ends]
        ybuf[:, cs] = _from_piece_order(jnp.concatenate(outs, axis=0)).astype(BF16)
        h_out_ref[0, :, cs] = carry[0:1, :]
        hcar[0, :, cs] = carry

    def conv_block(m, u):
        if not sample:
            return conv_block_pieces(m, u)
        cs = slice(m * MXU_COLS, (m + 1) * MXU_COLS)
        ucs = []
        for s in range(nb):
            ubuf[s, HIST - hist:HIST, cs] = sconv_ref[s, :, cs]
            ubuf[s, HIST:HIST + ts, cs] = u[s * ts:(s + 1) * ts]
            acc = u[s * ts:(s + 1) * ts] * cw_ref[hist:hist + 1, cs]
            for j in range(hist):
                acc = acc + ubuf[s, HIST - hist + j:HIST - hist + j + ts, cs] * cw_ref[j:j + 1, cs]
            ucs.append(acc + cb_ref[:, cs])
            conv_out_ref[s, :, cs] = ubuf[s, HIST + ts - hist:HIST + ts, cs]
        return jnp.concatenate(ucs, axis=0)

    def gate_dots(m, ucb):
        return [_dot(ucb[:, j * RNN_BLOCK:(j + 1) * RNN_BLOCK], wg_ref[m * per_block + j]) for j in range(per_block)]

    def gate_math(m, zs, uc):
        a_parts, b_parts = [], []
        for j, z in enumerate(zs):
            n = m * per_block + j
            sl = slice(n * RNN_BLOCK, (n + 1) * RNN_BLOCK)
            r = _sigmoid(z[:, :RNN_BLOCK] + ba_ref[:, sl])
            i = _sigmoid(z[:, RNN_BLOCK:] + bx_ref[:, sl])
            a = jnp.exp2(decay_log2[:, sl] * r)
            a_parts.append(a)
            b_parts.append(_sqrt_nonneg(1.0 - a * a) * (i * uc[:, j * RNN_BLOCK:(j + 1) * RNN_BLOCK]))
        return jnp.concatenate(a_parts, axis=1), jnp.concatenate(b_parts, axis=1)

    def scan_block(m, a_all, b_all):
        if not sample:
            return scan_block_pieces(m, a_all, b_all)
        cs = slice(m * MXU_COLS, (m + 1) * MXU_COLS)
        for s in range(nb):
            h0 = jnp.broadcast_to(sh_ref[s, :, cs], (SUBLANES, MXU_COLS))
            y, hl = _scan_rows(a_all[s * ts:(s + 1) * ts], b_all[s * ts:(s + 1) * ts], h0, row_id)
            ybuf[s * ts:(s + 1) * ts, cs] = y.astype(BF16)
            h_out_ref[s, :, cs] = hl[0:1, :]

    u_first = jnp.concatenate([_dot(h, w_in_ref[:, U_OFF:U_OFF + MXU_COLS]) for h in hb_halves], axis=0)
    project_q(0)
    uc = conv_block(0, u_first)
    for m in range(n_blocks):
        zs = gate_dots(m, uc.astype(BF16))
        u_next = u_proj(m + 1) if m + 1 < n_blocks else None
        for work in side_work[m]:
            work()
        a_all, b_all = gate_math(m, zs, uc)
        if u_next is not None:
            uc = conv_block(m + 1, u_next)
        else:
            for work in tail_work:
                work()
        scan_block(m, a_all, b_all)

    lk = WINDOW + lq
    col_id = lax.broadcasted_iota(jnp.int32, (lq, lk), 1)
    units = [(s, c, kv) for s in range(nb) for c in range(n_chunks) for kv in range(N_KV_HEADS)]
    n_pieces = D_MODEL // MXU_COLS
    wr_at = {(len(units) * p) // n_pieces: p for p in range(n_pieces)}

    def rnn_proj(p):
        ps = slice(p * MXU_COLS, (p + 1) * MXU_COLS)
        g_rnn = _sigmoid(glbuf[:, ps] + bgate_ref[:, ps])
        mixbuf[:, ps] = g_rnn * _dot(ybuf[...], wr_ref[:, ps])

    def scores(unit):
        s, c, kv = unit
        r0 = s * ts + c * lq
        kk = kbuf[s, c * lq:c * lq + lk, kv * HEAD_DIM:(kv + 1) * HEAD_DIM]
        qs = jnp.concatenate(
            [qbuf[r0:r0 + lq, (kv * GROUP + g) * HEAD_DIM:(kv * GROUP + g + 1) * HEAD_DIM]
             for g in range(GROUP)], axis=0)
        return _dot_nt(qs, kk)

    def softmax_pv(unit, sc):
        s, c, kv = unit
        r0 = s * ts + c * lq
        vv = vbuf[s, c * lq:c * lq + lk, kv * HEAD_DIM:(kv + 1) * HEAD_DIM]
        sc = sc * (ATTN_SCALE * LOG2E)
        ps = []
        dens = []
        for g in range(GROUP):
            hd = kv * GROUP + g
            sg = sc[g * lq:(g + 1) * lq] - bias_ref[hd]
            if (not sample) and c * lq < WINDOW:
                first_valid = jnp.where(t == 0, WINDOW - c * lq, 0)
                sg = jnp.where(col_id >= first_valid, sg, -jnp.inf)
            sink = sinks_ref[hd] * LOG2E
            mx = jnp.maximum(jnp.max(sg, axis=-1, keepdims=True), sink)
            p = jnp.exp2(sg - mx)
            dens.append(jnp.sum(p, axis=-1, keepdims=True) + jnp.exp2(sink - mx))
            ps.append(p.astype(BF16))
        o = _dot(jnp.concatenate(ps, axis=0), vv)
        for g in range(GROUP):
            hd = kv * GROUP + g
            yabuf[r0:r0 + lq, hd * HEAD_DIM:(hd + 1) * HEAD_DIM] = (o[g * lq:(g + 1) * lq] / dens[g]).astype(BF16)

    sc = scores(units[0])
    for idx, unit in enumerate(units):
        sc_next = scores(units[idx + 1]) if idx + 1 < len(units) else None
        if idx in wr_at:
            rnn_proj(wr_at[idx])
        softmax_pv(unit, sc)
        sc = sc_next
    if not sample:
        for s in range(nb):
            kbuf[s, 0:WINDOW, :] = kbuf[s, ts:ts + WINDOW, :]
            vbuf[s, 0:WINDOW, :] = vbuf[s, ts:ts + WINDOW, :]

    attn_proj = [_dot(yabuf[...], wa_ref[:, p * MXU_COLS:(p + 1) * MXU_COLS]) for p in range(n_pieces)]
    acc = None
    for p in range(n_pieces):
        ps = slice(p * MXU_COLS, (p + 1) * MXU_COLS)
        gs = slice(D_MODEL + p * MXU_COLS, D_MODEL + (p + 1) * MXU_COLS)
        g_attn = _sigmoid(glbuf[:, gs] + bgate_ref[:, gs])
        mixed = mixbuf[:, ps] + g_attn * attn_proj[p]
        d = _dot(mixed.astype(BF16), wo_ref[ps, :])
        acc = d if acc is None else acc + d
    x1_ref[...] = (x + acc).reshape(nb, ts, D_MODEL)


def _ffn_kernel(*refs, nb, ts, sample):
    if sample:
        (x_ref, sconv_ref, *rest) = refs
    else:
        (x_ref, *rest) = refs
    (g2_ref, wup_ref, cw_ref, cb_ref, wdn_ref, y_ref, conv_out_ref, cbuf, abuf, hbuf) = rest
    rows = nb * ts
    hist = FFN_CONV - 1
    t = pl.program_id(1)

    if not sample:
        @pl.when(t == 0)
        def _():
            cbuf[...] = jnp.zeros_like(cbuf)

    x = x_ref[...].reshape(rows, D_MODEL)
    hb = _rms_rows(x, g2_ref[...]).astype(BF16)
    for m in range(D_FF // MXU_COLS):
        cs = slice(m * MXU_COLS, (m + 1) * MXU_COLS)
        slot = m % 2
        a = _dot(hb, wup_ref[:, m * MXU_COLS:(m + 1) * MXU_COLS])
        b = _dot(hb, wup_ref[:, D_FF + m * MXU_COLS:D_FF + (m + 1) * MXU_COLS])
        acs = []
        for s in range(nb):
            if sample:
                abuf[slot, s, HIST - hist:HIST, :] = sconv_ref[s, :, cs]
            else:
                abuf[slot, s, 0:HIST, :] = cbuf[s, :, cs]
            abuf[slot, s, HIST:HIST + ts, :] = a[s * ts:(s + 1) * ts]
            acc = a[s * ts:(s + 1) * ts] * cw_ref[hist:hist + 1, cs]
            for j in range(hist):
                acc = acc + abuf[slot, s, HIST - hist + j:HIST - hist + j + ts, :] * cw_ref[j:j + 1, cs]
            acs.append(acc + cb_ref[:, cs])
            conv_out_ref[s, :, cs] = a[(s + 1) * ts - hist:(s + 1) * ts]
            if not sample:
                cbuf[s, :, cs] = a[(s + 1) * ts - HIST:(s + 1) * ts]
        ac = acs[0] if nb == 1 else jnp.concatenate(acs, axis=0)
        gelu = 0.5 * ac * (1.0 + lax.erf(ac * np.float32(math.sqrt(0.5))))
        hbuf[:, cs] = (gelu * b).astype(BF16)
    out = x + _dot(hbuf[...], wdn_ref[...])
    y_ref[...] = out.reshape(nb, ts, D_MODEL)


def _const_spec(shape):
    nd = len(shape)
    return pl.BlockSpec(shape, lambda b, t: (0,) * nd, pipeline_mode=pl.Buffered(1))


def _nbytes(shape, dtype):
    return math.prod(shape) * jnp.dtype(dtype).itemsize


def _check_vmem(name, scratch, tile_bytes, resident):
    need = (sum(_nbytes(s.shape, s.dtype) for s in scratch) + 2 * tile_bytes
            + sum(_nbytes(a.shape, a.dtype) for a in resident))
    assert need <= VMEM_LIMIT_BYTES, f'{name}: planned VMEM {need} B exceeds the limit {VMEM_LIMIT_BYTES} B'


def _alibi_bias(lq):
    slopes = np.array([2.0 ** (-8.0 * (h + 1) / N_HEADS) for h in range(N_HEADS)], dtype=np.float32)
    i = np.arange(lq, dtype=np.int32)[:, None]
    j = np.arange(WINDOW + lq, dtype=np.int32)[None, :]
    dist = np.abs(i + WINDOW - j).astype(np.float64)
    return jnp.asarray((slopes.astype(np.float64)[:, None, None] * dist[None] * LOG2E).astype(np.float32))


def _mixer_call(x, state, w, *, ts, sample):
    bsz, seq, _ = x.shape
    if sample:
        nb, grid, lq, n_chunks = bsz, (1, 1), seq, 1
    else:
        nb, grid, lq, n_chunks = 1, (bsz, seq // ts), CHUNK, ts // CHUNK
    rows = nb * ts
    tile = lambda b, t: (b, t, 0)
    per_b = lambda b, t: (b, 0, 0)
    in_specs = [pl.BlockSpec((nb, ts, D_MODEL), tile)]
    args = [x]
    if sample:
        sconv, sh, kc, vc = state
        in_specs += [pl.BlockSpec((nb, RNN_CONV - 1, D_RNN), per_b), pl.BlockSpec((nb, 1, D_RNN), per_b),
                     pl.BlockSpec((nb, WINDOW, KV_W), per_b), pl.BlockSpec((nb, WINDOW, KV_W), per_b)]
        args += [sconv, sh.reshape(bsz, 1, D_RNN), kc.reshape(bsz, WINDOW, KV_W), vc.reshape(bsz, WINDOW, KV_W)]
    weights = [w['g1'], w['w_in'], w['b_gate'], w['rnn_conv_w'], w['rnn_conv_b'], w['w_gate'], w['b_a'], w['b_x'],
               w['lam'], w['q_g'], w['k_g']]
    in_specs += [_const_spec(a.shape) for a in weights]
    args += weights
    in_specs.append(pl.BlockSpec(memory_space=pltpu.SMEM))
    args.append(w['sinks'])
    tail = [_alibi_bias(lq), w['w_rnn_proj'], w['w_attn_proj'], w['w_out']]
    in_specs += [_const_spec(a.shape) for a in tail]
    args += tail

    out_shape = (jax.ShapeDtypeStruct((bsz, seq, D_MODEL), F32),
                 jax.ShapeDtypeStruct((bsz, RNN_CONV - 1, D_RNN), F32),
                 jax.ShapeDtypeStruct((bsz, 1, D_RNN), F32),
                 jax.ShapeDtypeStruct((bsz, WINDOW, KV_W), F32),
                 jax.ShapeDtypeStruct((bsz, WINDOW, KV_W), F32))
    out_specs = (pl.BlockSpec((nb, ts, D_MODEL), tile),
                 pl.BlockSpec((nb, RNN_CONV - 1, D_RNN), per_b),
                 pl.BlockSpec((nb, 1, D_RNN), per_b),
                 pl.BlockSpec((nb, WINDOW, KV_W), per_b),
                 pl.BlockSpec((nb, WINDOW, KV_W), per_b))
    conv_hist = (nb, HIST + ts, D_RNN) if sample else (RNN_CONV - 1, SUBLANES, D_RNN)
    scratch = [pltpu.VMEM(conv_hist, F32),
               pltpu.VMEM((rows, D_RNN), BF16),
               pltpu.VMEM((nb, SUBLANES, D_RNN), F32),
               pltpu.VMEM((nb, WINDOW + ts, KV_W), BF16),
               pltpu.VMEM((nb, WINDOW + ts, KV_W), BF16),
               pltpu.VMEM((rows, Q_W), BF16),
               pltpu.VMEM((rows, Q_W), BF16),
               pltpu.VMEM((rows, 2 * D_MODEL), F32),
               pltpu.VMEM((rows, D_MODEL), F32)]
    _check_vmem('mixer', scratch, 2 * _nbytes((rows, D_MODEL), F32), args[1:])
    kern = functools.partial(_mixer_kernel, nb=nb, ts=ts, lq=lq, n_chunks=n_chunks, sample=sample)
    return pl.pallas_call(
        kern, out_shape=out_shape, grid=grid, in_specs=in_specs, out_specs=out_specs, scratch_shapes=scratch,
        name='mixer_sample' if sample else 'mixer_prompt',
        compiler_params=pltpu.CompilerParams(dimension_semantics=('arbitrary', 'arbitrary'),
                                             vmem_limit_bytes=VMEM_LIMIT_BYTES),
    )(*args)


def _ffn_call(x, state, w, *, ts, sample):
    bsz, seq, _ = x.shape
    if sample:
        nb, grid = bsz, (1, 1)
    else:
        nb, grid = 1, (bsz, seq // ts)
    rows = nb * ts
    tile = lambda b, t: (b, t, 0)
    per_b = lambda b, t: (b, 0, 0)
    in_specs = [pl.BlockSpec((nb, ts, D_MODEL), tile)]
    args = [x]
    if sample:
        in_specs.append(pl.BlockSpec((nb, FFN_CONV - 1, D_FF), per_b))
        args.append(state)
    weights = [w['g2'], w['w_up'], w['ffn_conv_w'], w['ffn_conv_b'], w['w_down']]
    in_specs += [_const_spec(a.shape) for a in weights]
    args += weights
    out_shape = (jax.ShapeDtypeStruct((bsz, seq, D_MODEL), F32),
                 jax.ShapeDtypeStruct((bsz, FFN_CONV - 1, D_FF), F32))
    out_specs = (pl.BlockSpec((nb, ts, D_MODEL), tile),
                 pl.BlockSpec((nb, FFN_CONV - 1, D_FF), per_b))
    scratch = [pltpu.VMEM((nb, HIST, D_FF), F32),
               pltpu.VMEM((2, nb, HIST + ts, MXU_COLS), F32),
               pltpu.VMEM((rows, D_FF), BF16)]
    _check_vmem('ffn', scratch, 2 * _nbytes((rows, D_MODEL), F32), args[1:])
    kern = functools.partial(_ffn_kernel, nb=nb, ts=ts, sample=sample)
    return pl.pallas_call(
        kern, out_shape=out_shape, grid=grid, in_specs=in_specs, out_specs=out_specs, scratch_shapes=scratch,
        name='ffn_sample' if sample else 'ffn_prompt',
        compiler_params=pltpu.CompilerParams(dimension_semantics=('arbitrary', 'arbitrary'),
                                             vmem_limit_bytes=VMEM_LIMIT_BYTES),
    )(*args)


PROMPT_TILE = 512
FFN_TILE = 1024


def kernel(x_prompt, x_sample, state_rnn_conv, state_rnn_h, cache_attn_k, cache_attn_v, state_ffn_conv, norm_mix_g, w_in, b_gate, rnn_conv_w, rnn_conv_b, rnn_gate_a_w, rnn_gate_a_b, rnn_gate_x_w, rnn_gate_x_b, rnn_lambda, q_norm_g, k_norm_g, attn_sinks, w_rnn_proj, w_attn_proj, w_out, norm_ffn_g, w_up, ffn_conv_w, ffn_conv_b, w_down):
    depth = w_in.shape[0]
    xp, xs = x_prompt, x_sample
    assert xp.shape[1] % PROMPT_TILE == 0 and xp.shape[1] % FFN_TILE == 0 and xp.shape[2] == D_MODEL
    assert xs.shape[1] % (2 * SUBLANES) == 0 and xs.shape[1] <= WINDOW and cache_attn_k.shape[2] == WINDOW
    outs = [[] for _ in range(10)]
    for l in range(depth):
        w = {
            'g1': norm_mix_g[l][None], 'w_in': w_in[l].astype(BF16), 'b_gate': b_gate[l][None],
            'rnn_conv_w': rnn_conv_w[l], 'rnn_conv_b': rnn_conv_b[l][None],
            'w_gate': jnp.concatenate([rnn_gate_a_w[l], rnn_gate_x_w[l]], axis=-1).astype(BF16),
            'b_a': rnn_gate_a_b[l][None], 'b_x': rnn_gate_x_b[l][None], 'lam': rnn_lambda[l][None],
            'q_g': q_norm_g[l][None], 'k_g': k_norm_g[l][None], 'sinks': attn_sinks[l],
            'w_rnn_proj': w_rnn_proj[l].astype(BF16), 'w_attn_proj': w_attn_proj[l].astype(BF16),
            'w_out': w_out[l].astype(BF16),
            'g2': norm_ffn_g[l][None], 'w_up': w_up[l].astype(BF16), 'ffn_conv_w': ffn_conv_w[l],
            'ffn_conv_b': ffn_conv_b[l][None], 'w_down': w_down[l].astype(BF16),
        }
        xp, conv_p, h_p, k_p, v_p = _mixer_call(xp, None, w, ts=PROMPT_TILE, sample=False)
        xp, fconv_p = _ffn_call(xp, None, w, ts=FFN_TILE, sample=False)
        xs, conv_s, h_s, k_s, v_s = _mixer_call(
            xs, (state_rnn_conv[l], state_rnn_h[l], cache_attn_k[l], cache_attn_v[l]), w,
            ts=xs.shape[1], sample=True)
        xs, fconv_s = _ffn_call(xs, state_ffn_conv[l], w, ts=xs.shape[1], sample=True)
        kv_shape = lambda a: a.reshape(a.shape[0], WINDOW, N_KV_HEADS, HEAD_DIM)
        for j, o in enumerate((conv_p, conv_s, h_p[:, 0], h_s[:, 0], kv_shape(k_p), kv_shape(k_s),
                               kv_shape(v_p), kv_shape(v_s), fconv_p, fconv_s)):
            outs[j].append(o)
    return (xp, xs) + tuple(jnp.stack(o) for o in outs)
```

```python
import functools
import math

import jax
import jax.numpy as jnp
import numpy as np
from jax import lax
from jax.experimental import pallas as pl
from jax.experimental.pallas import tpu as pltpu

D_MODEL = 1024
CHUNK = 64
WINDOW = 128
N_HEADS = 8
N_KV_HEADS = 2
HEAD_DIM = 128
GROUP = N_HEADS // N_KV_HEADS
ATTN_SCALE = HEAD_DIM ** -0.5
D_RNN = D_MODEL
N_RNN_BLOCKS = 8
RNN_BLOCK = D_RNN // N_RNN_BLOCKS
RNN_CONV = 4
RGLRU_C = 8.0
D_FF = ((8 * D_MODEL // 3 + 127) // 128) * 128
FFN_CONV = 3
EPS = 1e-6
Q_W = N_HEADS * HEAD_DIM
KV_W = N_KV_HEADS * HEAD_DIM
IN_W = D_RNN + Q_W + 2 * KV_W + 2 * D_MODEL
U_OFF, Q_OFF, K_OFF, V_OFF, G_OFF = 0, D_RNN, D_RNN + Q_W, D_RNN + Q_W + KV_W, D_RNN + Q_W + 2 * KV_W

SUBLANES = 8
MXU_COLS = 256
PHASES = SUBLANES
HIST = SUBLANES
VMEM_LIMIT_BYTES = 56 * 1024 * 1024
LOG2E = 1.4426950408889634
NEG_LOG2E = -LOG2E

F32 = jnp.float32
BF16 = jnp.bfloat16


def _sigmoid(x):
    return 1.0 / (1.0 + jnp.exp2(x * NEG_LOG2E))


def _sqrt_nonneg(y):
    return jnp.where(y > 0.0, y * lax.rsqrt(y), 0.0)


def _rms_rows(x, g):
    r = lax.rsqrt(jnp.mean(x * x, axis=-1, keepdims=True) + EPS)
    return x * r * g


def _dot(a, b):
    return jnp.dot(a, b, preferred_element_type=F32)


def _dot_nt(a, b):
    return lax.dot_general(a, b, (((1,), (1,)), ((), ())), preferred_element_type=F32)


def _scan_rows(a, b, h0, row_id):
    outs = []
    carry = h0
    for g in range(a.shape[0] // SUBLANES):
        ag = a[g * SUBLANES:(g + 1) * SUBLANES]
        bg = b[g * SUBLANES:(g + 1) * SUBLANES]
        for sh in (1, 2, 4):
            m = row_id >= sh
            bg = jnp.where(m, ag * pltpu.roll(bg, sh, 0) + bg, bg)
            ag = jnp.where(m, ag * pltpu.roll(ag, sh, 0), ag)
        h = ag * carry + bg
        outs.append(h)
        carry = jnp.broadcast_to(h[SUBLANES - 1:SUBLANES, :], h.shape)
    return (outs[0] if len(outs) == 1 else jnp.concatenate(outs, axis=0)), carry


def _row_groups(v):
    return [v[k * SUBLANES:(k + 1) * SUBLANES] for k in range(v.shape[0] // SUBLANES)]


def _shift_rows(v, prev_group, row_id):
    rolled = [pltpu.roll(g, 1, 0) for g in [prev_group] + _row_groups(v)]
    return jnp.concatenate([jnp.where(row_id >= 1, rolled[k + 1], rolled[k]) for k in range(len(rolled) - 1)], axis=0)


def _to_piece_order(v):
    n, c = v.shape
    return jnp.swapaxes(v.reshape(n // PHASES, PHASES, c), 0, 1).reshape(n, c)


def _from_piece_order(v):
    n, c = v.shape
    return jnp.swapaxes(v.reshape(PHASES, n // PHASES, c), 0, 1).reshape(n, c)


def _mixer_kernel(*refs, nb, ts, lq, n_chunks, sample):
    if sample:
        (x_ref, sconv_ref, sh_ref, kc_ref, vc_ref, *rest) = refs
    else:
        (x_ref, *rest) = refs
        kc_ref = vc_ref = None
    (g1_ref, w_in_ref, bgate_ref, cw_ref, cb_ref, wg_ref, ba_ref, bx_ref, lam_ref, qg_ref, kg_ref,
     sinks_ref, bias_ref, wr_ref, wa_ref, wo_ref,
     x1_ref, conv_out_ref, h_out_ref, k_out_ref, v_out_ref,
     ubuf, ybuf, hcar, kbuf, vbuf, qbuf, yabuf, glbuf, mixbuf, *window_bufs) = rest
    kwin, vwin = (None, None) if sample else window_bufs

    rows = nb * ts
    t = pl.program_id(1)

    if not sample:
        @pl.when(t == 0)
        def _():
            ubuf[...] = jnp.zeros_like(ubuf)
            hcar[...] = jnp.zeros_like(hcar)
            kbuf[:, 0:WINDOW, :] = jnp.zeros((nb, WINDOW, KV_W), BF16)
            vbuf[:, 0:WINDOW, :] = jnp.zeros((nb, WINDOW, KV_W), BF16)

    x = x_ref[...].reshape(rows, D_MODEL)
    half_rows = rows // 2
    hb_halves = [_rms_rows(x[i * half_rows:(i + 1) * half_rows], g1_ref[...]).astype(BF16) for i in range(2)]
    hb = jnp.concatenate(hb_halves, axis=0)

    lam = lam_ref[...]
    decay_log2 = (-RGLRU_C * LOG2E) * (jnp.maximum(-lam, 0.0) + jnp.log1p(jnp.exp(-jnp.abs(lam))))
    row_id = lax.broadcasted_iota(jnp.int32, (SUBLANES, MXU_COLS), 0)

    def project_q(half):
        q = _dot(hb, w_in_ref[:, Q_OFF + half * 512:Q_OFF + (half + 1) * 512])
        for j in range(4):
            hd = half * 4 + j
            qbuf[:, hd * HEAD_DIM:(hd + 1) * HEAD_DIM] = _rms_rows(
                q[:, j * HEAD_DIM:(j + 1) * HEAD_DIM], qg_ref[...]).astype(BF16)

    def project_kv():
        kvp = _dot(hb, w_in_ref[:, K_OFF:K_OFF + 2 * KV_W])
        kn = jnp.concatenate(
            [_rms_rows(kvp[:, j * HEAD_DIM:(j + 1) * HEAD_DIM], kg_ref[...]) for j in range(N_KV_HEADS)], axis=1)
        v = kvp[:, KV_W:2 * KV_W]
        for s in range(nb):
            for new, cache_ref, buf, out_ref, win in ((kn, kc_ref, kbuf, k_out_ref, kwin),
                                                      (v, vc_ref, vbuf, v_out_ref, vwin)):
                if sample:
                    for j in range(N_KV_HEADS):
                        hs = slice(j * HEAD_DIM, (j + 1) * HEAD_DIM)
                        old = cache_ref[s, :, j, :]
                        buf[s, 0:WINDOW, hs] = old.astype(BF16)
                        out_ref[s, 0:WINDOW - ts, j, :] = old[ts:WINDOW]
                        out_ref[s, WINDOW - ts:WINDOW, j, :] = new[s * ts:(s + 1) * ts, hs]
                else:
                    win[s] = new[(s + 1) * ts - WINDOW:(s + 1) * ts]
            kbuf[s, WINDOW:WINDOW + ts, :] = kn[s * ts:(s + 1) * ts].astype(BF16)
            vbuf[s, WINDOW:WINDOW + ts, :] = v[s * ts:(s + 1) * ts].astype(BF16)

    def project_gate(part):
        glbuf[:, part * 512:(part + 1) * 512] = _dot(
            hb, w_in_ref[:, G_OFF + part * 512:G_OFF + (part + 1) * 512])

    side_work = [[lambda: project_q(1)],
                 [project_kv, lambda: project_gate(0)],
                 [lambda: project_gate(1)],
                 [lambda: project_gate(2)]]
    tail_work = [lambda: project_gate(3)]

    n_blocks = D_RNN // MXU_COLS
    per_block = MXU_COLS // RNN_BLOCK

    def u_proj(m):
        return _dot(hb, w_in_ref[:, U_OFF + m * MXU_COLS:U_OFF + (m + 1) * MXU_COLS])

    g_rows = ts // PHASES
    hist = RNN_CONV - 1

    def conv_block_pieces(m, u):
        cs = slice(m * MXU_COLS, (m + 1) * MXU_COLS)
        u = _to_piece_order(u)
        shifted = []
        for k in range(hist):
            pc = PHASES - hist + k
            piece = u[pc * g_rows:(pc + 1) * g_rows]
            shifted.append(_shift_rows(piece, ubuf[k, :, cs], row_id))
            ubuf[k, :, cs] = piece[g_rows - SUBLANES:g_rows]
            conv_out_ref[0, k:k + 1, cs] = piece[g_rows - 1:g_rows]
        ext = jnp.concatenate(shifted + [u], axis=0)
        acc = ext[hist * g_rows:hist * g_rows + ts] * cw_ref[hist:hist + 1, cs]
        for k in range(hist):
            acc = acc + ext[k * g_rows:k * g_rows + ts] * cw_ref[k:k + 1, cs]
        return acc + cb_ref[:, cs]

    def scan_block_pieces(m, a_all, b_all):
        cs = slice(m * MXU_COLS, (m + 1) * MXU_COLS)
        h0 = hcar[0, :, cs]
        loc = b_all[0:g_rows]
        dec = a_all[0:g_rows]
        locs, decs = [loc], [dec]
        for j in range(1, PHASES):
            aj = a_all[j * g_rows:(j + 1) * g_rows]
            loc = aj * loc + b_all[j * g_rows:(j + 1) * g_rows]
            dec = aj * dec
            locs.append(loc)
            decs.append(dec)
        ends, carry = _scan_rows(dec, loc, h0, row_id)
        entry = _shift_rows(ends, h0, row_id)
        outs = [locs[j] + decs[j] * entry for j in range(PHASES - 1)] + [ends]
        ybuf[:, cs] = _from_piece_order(jnp.concatenate(outs, axis=0)).astype(BF16)
        h_out_ref[0, :, cs] = carry[0:1, :]
        hcar[0, :, cs] = carry

    def conv_block(m, u):
        if not sample:
            return conv_block_pieces(m, u)
        cs = slice(m * MXU_COLS, (m + 1) * MXU_COLS)
        ucs = []
        for s in range(nb):
            ubuf[s, HIST - hist:HIST, cs] = sconv_ref[s, :, cs]
            ubuf[s, HIST:HIST + ts, cs] = u[s * ts:(s + 1) * ts]
            acc = u[s * ts:(s + 1) * ts] * cw_ref[hist:hist + 1, cs]
            for j in range(hist):
                acc = acc + ubuf[s, HIST - hist + j:HIST - hist + j + ts, cs] * cw_ref[j:j + 1, cs]
            ucs.append(acc + cb_ref[:, cs])
            conv_out_ref[s, :, cs] = ubuf[s, HIST + ts - hist:HIST + ts, cs]
        return jnp.concatenate(ucs, axis=0)

    def gate_dots(m, ucb):
        return [_dot(ucb[:, j * RNN_BLOCK:(j + 1) * RNN_BLOCK], wg_ref[m * per_block + j]) for j in range(per_block)]

    def gate_math(m, zs, uc):
        a_parts, b_parts = [], []
        for j, z in enumerate(zs):
            n = m * per_block + j
            sl = slice(n * RNN_BLOCK, (n + 1) * RNN_BLOCK)
            r = _sigmoid(z[:, :RNN_BLOCK] + ba_ref[:, sl])
            i = _sigmoid(z[:, RNN_BLOCK:] + bx_ref[:, sl])
            a = jnp.exp2(decay_log2[:, sl] * r)
            a_parts.append(a)
            b_parts.append(_sqrt_nonneg(1.0 - a * a) * (i * uc[:, j * RNN_BLOCK:(j + 1) * RNN_BLOCK]))
        return jnp.concatenate(a_parts, axis=1), jnp.concatenate(b_parts, axis=1)

    def scan_block(m, a_all, b_all):
        if not sample:
            return scan_block_pieces(m, a_all, b_all)
        cs = slice(m * MXU_COLS, (m + 1) * MXU_COLS)
        for s in range(nb):
            h0 = jnp.broadcast_to(sh_ref[s, :, cs], (SUBLANES, MXU_COLS))
            y, hl = _scan_rows(a_all[s * ts:(s + 1) * ts], b_all[s * ts:(s + 1) * ts], h0, row_id)
            ybuf[s * ts:(s + 1) * ts, cs] = y.astype(BF16)
            h_out_ref[s, :, cs] = hl[0:1, :]

    u_first = jnp.concatenate([_dot(h, w_in_ref[:, U_OFF:U_OFF + MXU_COLS]) for h in hb_halves], axis=0)
    project_q(0)
    uc = conv_block(0, u_first)
    for m in range(n_blocks):
        zs = gate_dots(m, uc.astype(BF16))
        u_next = u_proj(m + 1) if m + 1 < n_blocks else None
        for work in side_work[m]:
            work()
        a_all, b_all = gate_math(m, zs, uc)
        if u_next is not None:
            uc = conv_block(m + 1, u_next)
        else:
            for work in tail_work:
                work()
        scan_block(m, a_all, b_all)

    lk = WINDOW + lq
    col_id = lax.broadcasted_iota(jnp.int32, (lq, lk), 1)
    units = [(s, c, kv) for s in range(nb) for c in range(n_chunks) for kv in range(N_KV_HEADS)]
    n_pieces = D_MODEL // MXU_COLS
    wr_at = {(len(units) * p) // n_pieces: p for p in range(n_pieces)}

    def rnn_proj(p):
        ps = slice(p * MXU_COLS, (p + 1) * MXU_COLS)
        g_rnn = _sigmoid(glbuf[:, ps] + bgate_ref[:, ps])
        mixbuf[:, ps] = g_rnn * _dot(ybuf[...], wr_ref[:, ps])

    def scores(unit):
        s, c, kv = unit
        r0 = s * ts + c * lq
        kk = kbuf[s, c * lq:c * lq + lk, kv * HEAD_DIM:(kv + 1) * HEAD_DIM]
        qs = jnp.concatenate(
            [qbuf[r0:r0 + lq, (kv * GROUP + g) * HEAD_DIM:(kv * GROUP + g + 1) * HEAD_DIM]
             for g in range(GROUP)], axis=0)
        return _dot_nt(qs, kk)

    def softmax_pv(unit, sc):
        s, c, kv = unit
        r0 = s * ts + c * lq
        vv = vbuf[s, c * lq:c * lq + lk, kv * HEAD_DIM:(kv + 1) * HEAD_DIM]
        sc = sc * (ATTN_SCALE * LOG2E)
        ps = []
        dens = []
        for g in range(GROUP):
            hd = kv * GROUP + g
            sg = sc[g * lq:(g + 1) * lq] - bias_ref[hd]
            if (not sample) and c * lq < WINDOW:
                first_valid = jnp.where(t == 0, WINDOW - c * lq, 0)
                sg = jnp.where(col_id >= first_valid, sg, -jnp.inf)
            sink = sinks_ref[hd] * LOG2E
            mx = jnp.maximum(jnp.max(sg, axis=-1, keepdims=True), sink)
            p = jnp.exp2(sg - mx)
            dens.append(jnp.sum(p, axis=-1, keepdims=True) + jnp.exp2(sink - mx))
            ps.append(p.astype(BF16))
        o = _dot(jnp.concatenate(ps, axis=0), vv)
        for g in range(GROUP):
            hd = kv * GROUP + g
            yabuf[r0:r0 + lq, hd * HEAD_DIM:(hd + 1) * HEAD_DIM] = (o[g * lq:(g + 1) * lq] / dens[g]).astype(BF16)

    sc = scores(units[0])
    for idx, unit in enumerate(units):
        sc_next = scores(units[idx + 1]) if idx + 1 < len(units) else None
        if idx in wr_at:
            rnn_proj(wr_at[idx])
        softmax_pv(unit, sc)
        sc = sc_next
    if not sample:
        for s in range(nb):
            kbuf[s, 0:WINDOW, :] = kbuf[s, ts:ts + WINDOW, :]
            vbuf[s, 0:WINDOW, :] = vbuf[s, ts:ts + WINDOW, :]

    attn_proj = [_dot(yabuf[...], wa_ref[:, p * MXU_COLS:(p + 1) * MXU_COLS]) for p in range(n_pieces)]
    acc = None
    for p in range(n_pieces):
        ps = slice(p * MXU_COLS, (p + 1) * MXU_COLS)
        gs = slice(D_MODEL + p * MXU_COLS, D_MODEL + (p + 1) * MXU_COLS)
        g_attn = _sigmoid(glbuf[:, gs] + bgate_ref[:, gs])
        mixed = mixbuf[:, ps] + g_attn * attn_proj[p]
        d = _dot(mixed.astype(BF16), wo_ref[ps, :])
        acc = d if acc is None else acc + d
    x1_ref[...] = (x + acc).reshape(nb, ts, D_MODEL)
    if not sample:
        @pl.when(t == pl.num_programs(1) - 1)
        def _():
            for s in range(nb):
                for j in range(N_KV_HEADS):
                    k_out_ref[s, :, j, :] = kwin[s, :, j * HEAD_DIM:(j + 1) * HEAD_DIM]
                    v_out_ref[s, :, j, :] = vwin[s, :, j * HEAD_DIM:(j + 1) * HEAD_DIM]


def _ffn_kernel(*refs, nb, ts, sample):
    if sample:
        (x_ref, sconv_ref, *rest) = refs
    else:
        (x_ref, *rest) = refs
    (g2_ref, wup_ref, cw_ref, cb_ref, wdn_ref, y_ref, conv_out_ref, cbuf, abuf, hbuf) = rest
    rows = nb * ts
    hist = FFN_CONV - 1
    t = pl.program_id(1)

    if not sample:
        @pl.when(t == 0)
        def _():
            cbuf[...] = jnp.zeros_like(cbuf)

    x = x_ref[...].reshape(rows, D_MODEL)
    hb = _rms_rows(x, g2_ref[...]).astype(BF16)
    for m in range(D_FF // MXU_COLS):
        cs = slice(m * MXU_COLS, (m + 1) * MXU_COLS)
        slot = m % 2
        a = _dot(hb, wup_ref[:, m * MXU_COLS:(m + 1) * MXU_COLS])
        b = _dot(hb, wup_ref[:, D_FF + m * MXU_COLS:D_FF + (m + 1) * MXU_COLS])
        acs = []
        for s in range(nb):
            if sample:
                abuf[slot, s, HIST - hist:HIST, :] = sconv_ref[s, :, cs]
            else:
                abuf[slot, s, 0:HIST, :] = cbuf[s, :, cs]
            abuf[slot, s, HIST:HIST + ts, :] = a[s * ts:(s + 1) * ts]
            acc = a[s * ts:(s + 1) * ts] * cw_ref[hist:hist + 1, cs]
            for j in range(hist):
                acc = acc + abuf[slot, s, HIST - hist + j:HIST - hist + j + ts, :] * cw_ref[j:j + 1, cs]
            acs.append(acc + cb_ref[:, cs])
            conv_out_ref[s, :, cs] = a[(s + 1) * ts - hist:(s + 1) * ts]
            if not sample:
                cbuf[s, :, cs] = a[(s + 1) * ts - HIST:(s + 1) * ts]
        ac = acs[0] if nb == 1 else jnp.concatenate(acs, axis=0)
        gelu = 0.5 * ac * (1.0 + lax.erf(ac * np.float32(math.sqrt(0.5))))
        hbuf[:, cs] = (gelu * b).astype(BF16)
    out = x + _dot(hbuf[...], wdn_ref[...])
    y_ref[...] = out.reshape(nb, ts, D_MODEL)


def _const_spec(shape):
    nd = len(shape)
    return pl.BlockSpec(shape, lambda b, t: (0,) * nd, pipeline_mode=pl.Buffered(1))


def _nbytes(shape, dtype):
    return math.prod(shape) * jnp.dtype(dtype).itemsize


def _check_vmem(name, scratch, tile_bytes, resident):
    need = (sum(_nbytes(s.shape, s.dtype) for s in scratch) + 2 * tile_bytes
            + sum(_nbytes(a.shape, a.dtype) for a in resident))
    assert need <= VMEM_LIMIT_BYTES, f'{name}: planned VMEM {need} B exceeds the limit {VMEM_LIMIT_BYTES} B'


def _alibi_bias(lq):
    slopes = np.array([2.0 ** (-8.0 * (h + 1) / N_HEADS) for h in range(N_HEADS)], dtype=np.float32)
    i = np.arange(lq, dtype=np.int32)[:, None]
    j = np.arange(WINDOW + lq, dtype=np.int32)[None, :]
    dist = np.abs(i + WINDOW - j).astype(np.float64)
    return jnp.asarray((slopes.astype(np.float64)[:, None, None] * dist[None] * LOG2E).astype(np.float32))


def _mixer_call(x, state, w, *, ts, sample):
    bsz, seq, _ = x.shape
    if sample:
        nb, grid, lq, n_chunks = bsz, (1, 1), seq, 1
    else:
        nb, grid, lq, n_chunks = 1, (bsz, seq // ts), CHUNK, ts // CHUNK
    rows = nb * ts
    tile = lambda b, t: (b, t, 0)
    per_b = lambda b, t: (b, 0, 0)
    kv_state_shape = (bsz, WINDOW, N_KV_HEADS, HEAD_DIM)
    kv_state_spec = pl.BlockSpec((nb, WINDOW, N_KV_HEADS, HEAD_DIM), lambda b, t: (b, 0, 0, 0))
    in_specs = [pl.BlockSpec((nb, ts, D_MODEL), tile)]
    args = [x]
    if sample:
        sconv, sh, kc, vc = state
        in_specs += [pl.BlockSpec((nb, RNN_CONV - 1, D_RNN), per_b), pl.BlockSpec((nb, 1, D_RNN), per_b),
                     kv_state_spec, kv_state_spec]
        args += [sconv, sh.reshape(bsz, 1, D_RNN), kc, vc]
    weights = [w['g1'], w['w_in'], w['b_gate'], w['rnn_conv_w'], w['rnn_conv_b'], w['w_gate'], w['b_a'], w['b_x'],
               w['lam'], w['q_g'], w['k_g']]
    in_specs += [_const_spec(a.shape) for a in weights]
    args += weights
    in_specs.append(pl.BlockSpec(memory_space=pltpu.SMEM))
    args.append(w['sinks'])
    tail = [_alibi_bias(lq), w['w_rnn_proj'], w['w_attn_proj'], w['w_out']]
    in_specs += [_const_spec(a.shape) for a in tail]
    args += tail

    out_shape = (jax.ShapeDtypeStruct((bsz, seq, D_MODEL), F32),
                 jax.ShapeDtypeStruct((bsz, RNN_CONV - 1, D_RNN), F32),
                 jax.ShapeDtypeStruct((bsz, 1, D_RNN), F32),
                 jax.ShapeDtypeStruct(kv_state_shape, F32),
                 jax.ShapeDtypeStruct(kv_state_shape, F32))
    out_specs = (pl.BlockSpec((nb, ts, D_MODEL), tile),
                 pl.BlockSpec((nb, RNN_CONV - 1, D_RNN), per_b),
                 pl.BlockSpec((nb, 1, D_RNN), per_b),
                 kv_state_spec, kv_state_spec)
    conv_hist = (nb, HIST + ts, D_RNN) if sample else (RNN_CONV - 1, SUBLANES, D_RNN)
    scratch = [pltpu.VMEM(conv_hist, F32),
               pltpu.VMEM((rows, D_RNN), BF16),
               pltpu.VMEM((nb, SUBLANES, D_RNN), F32),
               pltpu.VMEM((nb, WINDOW + ts, KV_W), BF16),
               pltpu.VMEM((nb, WINDOW + ts, KV_W), BF16),
               pltpu.VMEM((rows, Q_W), BF16),
               pltpu.VMEM((rows, Q_W), BF16),
               pltpu.VMEM((rows, 2 * D_MODEL), F32),
               pltpu.VMEM((rows, D_MODEL), F32)]
    if not sample:
        scratch += [pltpu.VMEM((nb, WINDOW, KV_W), F32),
                    pltpu.VMEM((nb, WINDOW, KV_W), F32)]
    _check_vmem('mixer', scratch, 2 * _nbytes((rows, D_MODEL), F32), args[1:])
    kern = functools.partial(_mixer_kernel, nb=nb, ts=ts, lq=lq, n_chunks=n_chunks, sample=sample)
    return pl.pallas_call(
        kern, out_shape=out_shape, grid=grid, in_specs=in_specs, out_specs=out_specs, scratch_shapes=scratch,
        name='mixer_sample' if sample else 'mixer_prompt',
        compiler_params=pltpu.CompilerParams(dimension_semantics=('arbitrary', 'arbitrary'),
                                             vmem_limit_bytes=VMEM_LIMIT_BYTES),
    )(*args)


def _ffn_call(x, state, w, *, ts, sample):
    bsz, seq, _ = x.shape
    if sample:
        nb, grid = bsz, (1, 1)
    else:
        nb, grid = 1, (bsz, seq // ts)
    rows = nb * ts
    tile = lambda b, t: (b, t, 0)
    per_b = lambda b, t: (b, 0, 0)
    in_specs = [pl.BlockSpec((nb, ts, D_MODEL), tile)]
    args = [x]
    if sample:
        in_specs.append(pl.BlockSpec((nb, FFN_CONV - 1, D_FF), per_b))
        args.append(state)
    weights = [w['g2'], w['w_up'], w['ffn_conv_w'], w['ffn_conv_b'], w['w_down']]
    in_specs += [_const_spec(a.shape) for a in weights]
    args += weights
    out_shape = (jax.ShapeDtypeStruct((bsz, seq, D_MODEL), F32),
                 jax.ShapeDtypeStruct((bsz, FFN_CONV - 1, D_FF), F32))
    out_specs = (pl.BlockSpec((nb, ts, D_MODEL), tile),
                 pl.BlockSpec((nb, FFN_CONV - 1, D_FF), per_b))
    scratch = [pltpu.VMEM((nb, HIST, D_FF), F32),
               pltpu.VMEM((2, nb, HIST + ts, MXU_COLS), F32),
               pltpu.VMEM((rows, D_FF), BF16)]
    _check_vmem('ffn', scratch, 2 * _nbytes((rows, D_MODEL), F32), args[1:])
    kern = functools.partial(_ffn_kernel, nb=nb, ts=ts, sample=sample)
    return pl.pallas_call(
        kern, out_shape=out_shape, grid=grid, in_specs=in_specs, out_specs=out_specs, scratch_shapes=scratch,
        name='ffn_sample' if sample else 'ffn_prompt',
        compiler_params=pltpu.CompilerParams(dimension_semantics=('arbitrary', 'arbitrary'),
                                             vmem_limit_bytes=VMEM_LIMIT_BYTES),
    )(*args)


PROMPT_TILE = 512
FFN_TILE = 1024


def kernel(x_prompt, x_sample, state_rnn_conv, state_rnn_h, cache_attn_k, cache_attn_v, state_ffn_conv, norm_mix_g, w_in, b_gate, rnn_conv_w, rnn_conv_b, rnn_gate_a_w, rnn_gate_a_b, rnn_gate_x_w, rnn_gate_x_b, rnn_lambda, q_norm_g, k_norm_g, attn_sinks, w_rnn_proj, w_attn_proj, w_out, norm_ffn_g, w_up, ffn_conv_w, ffn_conv_b, w_down):
    depth = w_in.shape[0]
    xp, xs = x_prompt, x_sample
    assert xp.shape[1] % PROMPT_TILE == 0 and xp.shape[1] % FFN_TILE == 0 and xp.shape[2] == D_MODEL
    assert xs.shape[1] % (2 * SUBLANES) == 0 and xs.shape[1] <= WINDOW and cache_attn_k.shape[2] == WINDOW
    outs = [[] for _ in range(10)]
    for l in range(depth):
        w = {
            'g1': norm_mix_g[l][None], 'w_in': w_in[l].astype(BF16), 'b_gate': b_gate[l][None],
            'rnn_conv_w': rnn_conv_w[l], 'rnn_conv_b': rnn_conv_b[l][None],
            'w_gate': jnp.concatenate([rnn_gate_a_w[l], rnn_gate_x_w[l]], axis=-1).astype(BF16),
            'b_a': rnn_gate_a_b[l][None], 'b_x': rnn_gate_x_b[l][None], 'lam': rnn_lambda[l][None],
            'q_g': q_norm_g[l][None], 'k_g': k_norm_g[l][None], 'sinks': attn_sinks[l],
            'w_rnn_proj': w_rnn_proj[l].astype(BF16), 'w_attn_proj': w_attn_proj[l].astype(BF16),
            'w_out': w_out[l].astype(BF16),
            'g2': norm_ffn_g[l][None], 'w_up': w_up[l].astype(BF16), 'ffn_conv_w': ffn_conv_w[l],
            'ffn_conv_b': ffn_conv_b[l][None], 'w_down': w_down[l].astype(BF16),
        }
        xp, conv_p, h_p, k_p, v_p = _mixer_call(xp, None, w, ts=PROMPT_TILE, sample=False)
        xp, fconv_p = _ffn_call(xp, None, w, ts=FFN_TILE, sample=False)
        xs, conv_s, h_s, k_s, v_s = _mixer_call(
            xs, (state_rnn_conv[l], state_rnn_h[l], cache_attn_k[l], cache_attn_v[l]), w,
            ts=xs.shape[1], sample=True)
        xs, fconv_s = _ffn_call(xs, state_ffn_conv[l], w, ts=xs.shape[1], sample=True)
        for j, o in enumerate((conv_p, conv_s, h_p[:, 0], h_s[:, 0], k_p, k_s, v_p, v_s, fconv_p, fconv_s)):
            outs[j].append(o)
    return (xp, xs) + tuple(jnp.stack(o) for o in outs)
```
